```python
import jax, jax.numpy as jnp
from jax import lax
import numpy as np

D_MODEL = 1024
BATCH = 16
SEQ = 4096
DEPTH = 1
DEC_BATCH = 128
DEC_SEQ = 1
PAST_LEN = 8192
PAGE_SIZE = 128

N_META = 16
POOL_WIDTH = D_MODEL // 2
POOL_WINDOWS = (2, 4, 8, 16)
N_POOL_GROUPS = len(POOL_WINDOWS)
POOL_GROUP = POOL_WIDTH // N_POOL_GROUPS
POOL_STATE = max(POOL_WINDOWS) - 1
ATTN_WIDTH = D_MODEL - POOL_WIDTH
HEAD_DIM = 64
N_HEADS = ATTN_WIDTH // HEAD_DIM
N_IDX_HEADS = 8
IDX_DIM = 64
TOPK_MAX = 256
QUERY_BLOCK = 128
N_EXPERT_GROUPS = 4
EXPERTS_PER_GROUP = 4
N_EXPERTS = N_EXPERT_GROUPS * EXPERTS_PER_GROUP
EXPERT_TOPK = 2
EXPERT_FF = 256
EPS = 1e-6

OFF_Q = POOL_WIDTH
OFF_K = OFF_Q + ATTN_WIDTH
OFF_V = OFF_K + ATTN_WIDTH
OFF_QI = OFF_V + ATTN_WIDTH
OFF_KI = OFF_QI + N_IDX_HEADS * IDX_DIM
OFF_W = OFF_KI + IDX_DIM
IN_COLS = OFF_W + N_IDX_HEADS

kernel_name = "hymba_pool_dsa_hiermoe_step"


def rmsnorm(x, g):
    xf = x.astype(jnp.float32)
    y = xf * lax.rsqrt(jnp.mean(xf * xf, axis=-1, keepdims=True) + EPS)
    return (y * g.astype(jnp.float32)).astype(x.dtype)


def project(xn, w_in, q_gain, k_gain, ki_gain):
    h = xn @ w_in
    B, T = h.shape[0], h.shape[1]
    u = h[..., :OFF_Q]
    q = rmsnorm(h[..., OFF_Q:OFF_K].reshape(B, T, N_HEADS, HEAD_DIM), q_gain)
    k = rmsnorm(h[..., OFF_K:OFF_V].reshape(B, T, N_HEADS, HEAD_DIM), k_gain)
    v = h[..., OFF_V:OFF_QI].reshape(B, T, N_HEADS, HEAD_DIM)
    qi = h[..., OFF_QI:OFF_KI].reshape(B, T, N_IDX_HEADS, IDX_DIM)
    ki = rmsnorm(h[..., OFF_KI:OFF_W], ki_gain)
    wi = h[..., OFF_W:] * (N_IDX_HEADS ** -0.5)
    return u, q, k, v, qi, ki, wi


def indexer_scores(qi, wi, ki):
    dots = jnp.einsum('bqhd,bsd->bqhs', qi, ki).astype(jnp.float32) * (IDX_DIM ** -0.5)
    return jnp.einsum('bqhs,bqh->bqs', jax.nn.relu(dots), wi.astype(jnp.float32))


def sparse_attend(q, kg, vg, valid):
    logits = jnp.einsum('bqhd,bqkhd->bqhk', q, kg).astype(jnp.float32) * (HEAD_DIM ** -0.5)
    logits = jnp.where(valid[:, :, None, :], logits, -jnp.inf)
    p = jax.nn.softmax(logits, axis=-1).astype(vg.dtype)
    return jnp.einsum('bqhk,bqkhd->bqhd', p, vg)


def multiscale_pool(u_ext, n_prev, pool_w, pool_scale):
    B, R, _ = u_ext.shape
    c = jnp.cumsum(u_ext.astype(jnp.float32), axis=1)
    cz = jnp.concatenate([jnp.zeros_like(c[:, :1]), c], axis=1)
    rows = np.arange(n_prev, R)
    outs = []
    for g, w in enumerate(POOL_WINDOWS):
        sl = slice(g * POOL_GROUP, (g + 1) * POOL_GROUP)
        hi = rows + 1
        lo = np.maximum(rows + 1 - w, 0)
        cnt = jnp.asarray((hi - lo)[None, :, None], jnp.float32)
        mean = (cz[:, hi, sl] - cz[:, lo, sl]) / cnt
        outs.append(mean - u_ext[:, n_prev:, sl].astype(jnp.float32))
    m = jnp.stack(outs, axis=2).astype(u_ext.dtype)
    y = jnp.einsum('brgc,gcd->brgd', m, pool_w).reshape(B, R - n_prev, POOL_WIDTH)
    return y * pool_scale


def prompt_mixer(xn, w_in, w_out, q_gain, k_gain, ki_gain, pool_w, pool_scale):
    u, q, k, v, qi, ki, wi = project(xn, w_in, q_gain, k_gain, ki_gain)
    B, T = xn.shape[0], xn.shape[1]
    y_pool = multiscale_pool(u, 0, pool_w, pool_scale)
    topk = min(TOPK_MAX, T // 4)
    nb = -(-T // QUERY_BLOCK)
    tp = nb * QUERY_BLOCK
    q_p = jnp.pad(q, ((0, 0), (0, tp - T), (0, 0), (0, 0)))
    qi_p = jnp.pad(qi, ((0, 0), (0, tp - T), (0, 0), (0, 0)))
    wi_p = jnp.pad(wi, ((0, 0), (0, tp - T), (0, 0)))
    key_pos = jnp.arange(T)

    def block(i):
        s0 = i * QUERY_BLOCK
        qb = lax.dynamic_slice_in_dim(q_p, s0, QUERY_BLOCK, axis=1)
        qib = lax.dynamic_slice_in_dim(qi_p, s0, QUERY_BLOCK, axis=1)
        wib = lax.dynamic_slice_in_dim(wi_p, s0, QUERY_BLOCK, axis=1)
        qpos = s0 + jnp.arange(QUERY_BLOCK)
        sc = indexer_scores(qib, wib, ki)
        sc = jnp.where((key_pos[None, :] <= qpos[:, None])[None], sc, -jnp.inf)
        _, idx = lax.top_k(sc, topk)
        kg = jax.vmap(lambda kk, ii: kk[ii])(k, idx)
        vg = jax.vmap(lambda vv, ii: vv[ii])(v, idx)
        valid = idx <= qpos[None, :, None]
        return sparse_attend(qb, kg, vg, valid)

    o = lax.map(block, jnp.arange(nb))
    o = jnp.moveaxis(o, 0, 1).reshape(B, tp, ATTN_WIDTH)[:, :T]
    mix = jnp.concatenate([y_pool, o], axis=-1) @ w_out
    return mix, k, v, ki, u[:, T - POOL_STATE:]


def sample_mixer(xn, l, cache_k, cache_v, cache_kidx, pool_state, page_table,
                 w_in, w_out, q_gain, k_gain, ki_gain, pool_w, pool_scale):
    u, q, k, v, qi, ki, wi = project(xn, w_in, q_gain, k_gain, ki_gain)
    DB, S = xn.shape[0], xn.shape[1]
    L = PAST_LEN + S
    topk = min(TOPK_MAX, L // 4)
    u_ext = jnp.concatenate([pool_state.astype(u.dtype), u], axis=1)
    y_pool = multiscale_pool(u_ext, POOL_STATE, pool_w, pool_scale)
    new_pool = u_ext[:, -POOL_STATE:]
    ki_past = cache_kidx[l, page_table].reshape(DB, PAST_LEN, IDX_DIM)
    ki_all = jnp.concatenate([ki_past.astype(ki.dtype), ki], axis=1)
    sc = indexer_scores(qi, wi, ki_all)
    qpos = PAST_LEN + jnp.arange(S)
    key_pos = jnp.arange(L)
    sc = jnp.where((key_pos[None, :] <= qpos[:, None])[None], sc, -jnp.inf)
    _, idx = lax.top_k(sc, topk)
    in_past = idx < PAST_LEN
    pidx = jnp.minimum(idx, PAST_LEN - 1)
    phys = page_table[jnp.arange(DB)[:, None, None], pidx // PAGE_SIZE]
    off = pidx % PAGE_SIZE
    nidx = jnp.clip(idx - PAST_LEN, 0, S - 1)
    kg = jnp.where(in_past[..., None, None], cache_k[l, phys, off].astype(k.dtype),
                   jax.vmap(lambda kk, ii: kk[ii])(k, nidx))
    vg = jnp.where(in_past[..., None, None], cache_v[l, phys, off].astype(v.dtype),
                   jax.vmap(lambda vv, ii: vv[ii])(v, nidx))
    valid = idx <= qpos[None, :, None]
    o = sparse_attend(q, kg, vg, valid).reshape(DB, S, ATTN_WIDTH)
    mix = jnp.concatenate([y_pool, o], axis=-1) @ w_out
    return mix, k, v, ki, new_pool


def hierarchical_moe(xn, w_rg, b_rg, w_re, b_re, w_gate, w_up, w_down):
    B, T, D = xn.shape
    xf = xn.reshape(-1, D)
    pg = jax.nn.softmax((xf @ w_rg + b_rg).astype(jnp.float32), axis=-1)
    g_sel = jnp.argmax(pg, axis=-1)
    p_sel = jnp.take_along_axis(pg, g_sel[:, None], axis=-1)
    el = (xf @ w_re + b_re).astype(jnp.float32).reshape(-1, N_EXPERT_GROUPS, EXPERTS_PER_GROUP)
    el_sel = jnp.take_along_axis(el, g_sel[:, None, None], axis=1)[:, 0]
    top_v, top_i = lax.top_k(el_sel, EXPERT_TOPK)
    w2 = jax.nn.softmax(top_v, axis=-1) * p_sel
    e_idx = g_sel[:, None] * EXPERTS_PER_GROUP + top_i
    gate = jnp.sum(jax.nn.one_hot(e_idx, N_EXPERTS, dtype=jnp.float32) * w2[..., None], axis=1)
    y = jnp.zeros(xf.shape, jnp.float32)
    for e in range(N_EXPERTS):
        h = jax.nn.silu(xf @ w_gate[e]) * (xf @ w_up[e])
        y = y + gate[:, e:e + 1] * (h @ w_down[e]).astype(jnp.float32)
    return y.astype(xn.dtype).reshape(B, T, D)


def setup_inputs(seed: int = 0) -> dict:
    key = jax.random.key(seed)
    ks = jax.random.split(key, 26)
    f32 = jnp.float32
    n_pages = PAST_LEN // PAGE_SIZE
    n_used = DEC_BATCH * n_pages
    n_pool = (5 * n_used) // 4

    def nrm(k, shape, s):
        return jax.random.normal(k, shape, f32) * s

    page_table = jax.random.permutation(ks[0], n_pool)[:n_used].reshape(DEC_BATCH, n_pages).astype(jnp.int32)
    return {
        "x_prompt": nrm(ks[1], (BATCH, SEQ, D_MODEL), 1.0),
        "x_sample": nrm(ks[2], (DEC_BATCH, DEC_SEQ, D_MODEL), 1.0),
        "cache_k": nrm(ks[3], (DEPTH, n_pool, PAGE_SIZE, N_HEADS, HEAD_DIM), 1.0),
        "cache_v": nrm(ks[4], (DEPTH, n_pool, PAGE_SIZE, N_HEADS, HEAD_DIM), 1.0),
        "cache_kidx": nrm(ks[5], (DEPTH, n_pool, PAGE_SIZE, IDX_DIM), 1.0),
        "state_pool": nrm(ks[6], (DEPTH, DEC_BATCH, POOL_STATE, POOL_WIDTH), 1.0),
        "page_table": page_table,
        "meta_tokens": nrm(ks[7], (N_META, D_MODEL), 1.0),
        "w_in": nrm(ks[8], (DEPTH, D_MODEL, IN_COLS), D_MODEL ** -0.5),
        "w_out": nrm(ks[9], (DEPTH, D_MODEL, D_MODEL), D_MODEL ** -0.5),
        "attn_norm": 1.0 + nrm(ks[10], (DEPTH, D_MODEL), 0.1),
        "ffn_norm": 1.0 + nrm(ks[11], (DEPTH, D_MODEL), 0.1),
        "q_norm": 1.0 + nrm(ks[12], (DEPTH, HEAD_DIM), 0.1),
        "k_norm": 1.0 + nrm(ks[13], (DEPTH, HEAD_DIM), 0.1),
        "kidx_norm": 1.0 + nrm(ks[14], (DEPTH, IDX_DIM), 0.1),
        "pool_w": nrm(ks[15], (DEPTH, N_POOL_GROUPS, POOL_GROUP, POOL_GROUP), POOL_GROUP ** -0.5),
        "pool_scale": 1.0 + nrm(ks[16], (DEPTH, POOL_WIDTH), 0.1),
        "router_group_w": nrm(ks[17], (DEPTH, D_MODEL, N_EXPERT_GROUPS), D_MODEL ** -0.5),
        "router_group_b": nrm(ks[18], (DEPTH, N_EXPERT_GROUPS), 0.01),
        "router_expert_w": nrm(ks[19], (DEPTH, D_MODEL, N_EXPERTS), D_MODEL ** -0.5),
        "router_expert_b": nrm(ks[20], (DEPTH, N_EXPERTS), 0.01),
        "expert_w_gate": nrm(ks[21], (DEPTH, N_EXPERTS, D_MODEL, EXPERT_FF), D_MODEL ** -0.5),
        "expert_w_up": nrm(ks[22], (DEPTH, N_EXPERTS, D_MODEL, EXPERT_FF), D_MODEL ** -0.5),
        "expert_w_down": nrm(ks[23], (DEPTH, N_EXPERTS, EXPERT_FF, D_MODEL), EXPERT_FF ** -0.5),
    }


def reference(x_prompt, x_sample, cache_k, cache_v, cache_kidx, state_pool, page_table,
              meta_tokens, w_in, w_out, attn_norm, ffn_norm, q_norm, k_norm, kidx_norm,
              pool_w, pool_scale, router_group_w, router_group_b, router_expert_w,
              router_expert_b, expert_w_gate, expert_w_up, expert_w_down):
    meta = jnp.broadcast_to(meta_tokens.astype(x_prompt.dtype)[None], (x_prompt.shape[0], N_META, D_MODEL))
    xp = jnp.concatenate([meta, x_prompt], axis=1)
    xs = x_sample
    kp_l, vp_l, kip_l, pp_l = [], [], [], []
    ks_l, vs_l, kis_l, ps_l = [], [], [], []
    for l in range(DEPTH):
        mix, kp, vp, kip, pp = prompt_mixer(rmsnorm(xp, attn_norm[l]), w_in[l], w_out[l], q_norm[l],
                                            k_norm[l], kidx_norm[l], pool_w[l], pool_scale[l])
        xp = xp + mix
        xp = xp + hierarchical_moe(rmsnorm(xp, ffn_norm[l]), router_group_w[l], router_group_b[l],
                                   router_expert_w[l], router_expert_b[l], expert_w_gate[l],
                                   expert_w_up[l], expert_w_down[l])
        mix, ksn, vsn, kis, ps = sample_mixer(rmsnorm(xs, attn_norm[l]), l, cache_k, cache_v, cache_kidx,
                                              state_pool[l], page_table, w_in[l], w_out[l], q_norm[l],
                                              k_norm[l], kidx_norm[l], pool_w[l], pool_scale[l])
        xs = xs + mix
        xs = xs + hierarchical_moe(rmsnorm(xs, ffn_norm[l]), router_group_w[l], router_group_b[l],
                                   router_expert_w[l], router_expert_b[l], expert_w_gate[l],
                                   expert_w_up[l], expert_w_down[l])
        kp_l.append(kp); vp_l.append(vp); kip_l.append(kip); pp_l.append(pp)
        ks_l.append(ksn); vs_l.append(vsn); kis_l.append(kis); ps_l.append(ps)
    y_prompt = xp[:, N_META:]
    y_sample = xs
    new_k_prompt = jnp.stack(kp_l)
    new_v_prompt = jnp.stack(vp_l)
    new_kidx_prompt = jnp.stack(kip_l)
    new_pool_prompt = jnp.stack(pp_l)
    new_k_sample = jnp.stack(ks_l)
    new_v_sample = jnp.stack(vs_l)
    new_kidx_sample = jnp.stack(kis_l)
    new_pool_sample = jnp.stack(ps_l)
    return (y_prompt, y_sample, new_k_prompt, new_v_prompt, new_kidx_prompt, new_pool_prompt,
            new_k_sample, new_v_sample, new_kidx_sample, new_pool_sample)
```

```python
import functools

import jax
import jax.numpy as jnp
from jax import lax
from jax.experimental import pallas as pl
from jax.experimental.pallas import tpu as pltpu

F32 = jnp.float32
BF16 = jnp.bfloat16
I32 = jnp.int32

N_META = 16
POOL_WINDOWS = (2, 4, 8, 16)
HEAD_DIM = 64
N_IDX_HEADS = 8
IDX_DIM = 64
TOPK_MAX = 256
N_EXPERT_GROUPS = 4
EXPERTS_PER_GROUP = 4
EPS = 1e-6

LANES = 128
INT_MIN = -(2 ** 31)
NEG_BIG = -1e30
VMEM_LIMIT = 56 * 1024 * 1024


def _cparams(sem):
    return pltpu.CompilerParams(dimension_semantics=sem, vmem_limit_bytes=VMEM_LIMIT)


def _nt_dot(a, b):
    return lax.dot_general(a, b, (((1,), (1,)), ((), ())), preferred_element_type=F32)


def _dot(a, b):
    return jnp.dot(a, b, preferred_element_type=F32)


def _group_mean_sq(hx, g_ref):
    sq = hx * hx
    hi = sq.astype(BF16)
    lo = (sq - hi.astype(F32)).astype(BF16)
    g = g_ref[...]
    halves = []
    for j in range(2):
        sl = slice(j * 256, (j + 1) * 256)
        halves.append(_dot(hi[:, sl], g) + _dot(lo[:, sl], g))
    return jnp.concatenate(halves, axis=1)


def _proj_kernel(x_ref, g_ref, w_ref, qg_ref, kg_ref, kig_ref, gm_ref, pw_ref, ps_ref, uh_ref,
                 yp_ref, q_ref, kb_ref, vb_ref, qi_ref, ki2_ref, wi_ref, kf_ref, vf_ref, kif_ref, ul_ref,
                 uext_ref, *, with_pool):
    i = pl.program_id(1)
    tm = x_ref.shape[1]
    x = x_ref[0]
    ms = jnp.mean(x * x, axis=-1, keepdims=True)
    xn = (x * lax.rsqrt(ms + EPS) * g_ref[...]).astype(BF16)
    h = _dot(xn, w_ref[...])

    u = h[:, 0:512]
    if with_pool:
        @pl.when(i == 0)
        def _():
            uext_ref[0:16, :] = uh_ref[...]

        uext_ref[16:16 + tm, :] = u
        for g, w in enumerate(POOL_WINDOWS):
            cs = slice(g * LANES, (g + 1) * LANES)
            acc = u[:, cs]
            for j in range(1, w):
                acc = acc + uext_ref[16 - j:16 - j + tm, cs]
            m = (acc * (1.0 / w) - u[:, cs]).astype(BF16)
            y = _dot(m, pw_ref[g]) * ps_ref[:, cs]
            yp_ref[0, :, cs] = y.astype(BF16)
        ul_ref[0] = uext_ref[tm:tm + 16, :]
        uext_ref[0:16, :] = uext_ref[tm:tm + 16, :]
    else:
        yp_ref[0] = u
        ul_ref[0] = jnp.zeros(ul_ref.shape[1:], F32)

    hq = h[:, 512:1024]
    qn = hq * lax.rsqrt(_group_mean_sq(hq, gm_ref) + EPS) * qg_ref[...]
    q_ref[0] = (qn * (HEAD_DIM ** -0.5)).astype(BF16)

    hk = h[:, 1024:1536]
    kn = hk * lax.rsqrt(_group_mean_sq(hk, gm_ref) + EPS) * kg_ref[...]
    kf_ref[0] = kn
    kb_ref[0] = kn.astype(BF16)

    hv = h[:, 1536:2048]
    vf_ref[0] = hv
    vb_ref[0] = hv.astype(BF16)

    qi_ref[0] = (h[:, 2048:2560] * (IDX_DIM ** -0.5)).astype(BF16)

    hki = h[:, 2560:2688]
    kin = hki * lax.rsqrt(jnp.mean(hki * hki, axis=-1, keepdims=True) + EPS) * kig_ref[...]
    kif_ref[0] = kin[:, 0:IDX_DIM]
    ki2_ref[0] = kin.astype(BF16)

    wi_ref[0] = h[:, 2688:2816] * (N_IDX_HEADS ** -0.5)


def _project(x, p, u_hist, *, tm, with_pool):
    B, T, D = x.shape
    nt = T // tm
    tile = lambda w: pl.BlockSpec((1, tm, w), lambda b, i: (b, i, 0))
    const2 = lambda a: pl.BlockSpec(a.shape, lambda b, i: (0,) * a.ndim)
    ins = [x, p["attn_g"], p["w_in"], p["q_g"], p["k_g"], p["ki_g"], p["gmat"], p["pool_w"], p["pool_s"], u_hist]
    in_specs = [tile(D)] + [const2(a) for a in ins[1:]]
    sds = jax.ShapeDtypeStruct
    out_shape = [
        sds((B, T, 512), BF16 if with_pool else F32),
        sds((B, T, 512), BF16),
        sds((B, T, 512), BF16),
        sds((B, T, 512), BF16),
        sds((B, T, 512), BF16),
        sds((B, T, 128), BF16),
        sds((B, T, 128), F32),
        sds((B, T, 512), F32),
        sds((B, T, 512), F32),
        sds((B, T, IDX_DIM), F32),
        sds((B, 16, 512), F32),
    ]
    out_specs = [tile(512)] * 5 + [tile(128), tile(128), tile(512), tile(512), tile(IDX_DIM),
                                   pl.BlockSpec((1, 16, 512), lambda b, i: (b, 0, 0))]
    names = ["yp", "q", "kb", "vb", "qi", "ki2", "wi", "kf", "vf", "kif", "ulast"]
    outs = pl.pallas_call(
        functools.partial(_proj_kernel, with_pool=with_pool),
        grid=(B, nt),
        in_specs=in_specs,
        out_specs=out_specs,
        out_shape=out_shape,
        scratch_shapes=[pltpu.VMEM((tm + 16, 512), F32)],
        compiler_params=_cparams(("arbitrary", "arbitrary")),
        name="proj",
    )(*ins)
    return dict(zip(names, outs))


def _sortable_key(s):
    bits = pltpu.bitcast(s, I32)
    bits = jnp.where(bits == INT_MIN, 0, bits)
    return bits ^ ((bits >> 31) & 0x7FFFFFFF)


def _select_topk(key_ref, bias_ref, nc, topk, tq, kc):
    def count(pred):
        def body(c, acc):
            return acc + jnp.where(pred(c, key_ref[c]), 1.0, 0.0)

        acc = lax.fori_loop(0, nc, body, jnp.zeros((tq, kc), F32))
        return jnp.sum(acc, axis=1, keepdims=True)

    kf = float(topk)

    def bit_body(j, t_u):
        cand_u = t_u | jnp.left_shift(jnp.int32(1), 31 - j)
        cand_s = cand_u ^ INT_MIN
        cnt = count(lambda c, k: k >= cand_s)
        return jnp.where(cnt >= kf, cand_u, t_u)

    t_u = lax.fori_loop(0, 32, bit_body, jnp.zeros((tq, 1), I32))
    t = t_u ^ INT_MIN
    cnt_ge = count(lambda c, k: k >= t)
    has_tie = jnp.max(jnp.where(t > INT_MIN, cnt_ge, 0.0)) > kf

    def write_bias(sel_fn):
        def body(c, carry):
            k = key_ref[c]
            bias_ref[c] = jnp.where(sel_fn(c, k) & (k > INT_MIN), 0.0, NEG_BIG)
            return carry

        lax.fori_loop(0, nc, body, 0)

    @pl.when(jnp.logical_not(has_tie))
    def _():
        write_bias(lambda c, k: k >= t)

    @pl.when(has_tie)
    def _():
        lane = lax.broadcasted_iota(I32, (tq, kc), 1)
        need = kf - count(lambda c, k: k > t)

        def pos_body(j, p):
            cand = p | jnp.left_shift(jnp.int32(1), 15 - j)
            cnt = count(lambda c, k: (k == t) & (c * kc + lane < cand))
            return jnp.where(cnt <= need, cand, p)

        p = lax.fori_loop(0, 16, pos_body, jnp.zeros((tq, 1), I32))
        write_bias(lambda c, k: (k > t) | ((k == t) & (c * kc + lane < p)))


def _attn_kernel(q_ref, qi_ref, wi_ref, kb_ref, vb_ref, ki_ref, km_ref, vm_ref, kim_ref,
                 o_ref, key_ref, bias_ref, qs_ref, *, topk, n_meta):
    i = pl.program_id(1)
    tq = q_ref.shape[1]
    kc = tq
    lane = lax.broadcasted_iota(I32, (tq, LANES), 1)
    lo_half = lane < HEAD_DIM
    row = lax.broadcasted_iota(I32, (tq, kc), 0)
    col = lax.broadcasted_iota(I32, (tq, kc), 1)

    for h in range(N_IDX_HEADS):
        blk = qi_ref[0, :, (h // 2) * LANES:(h // 2 + 1) * LANES]
        keep = lo_half if h % 2 == 0 else jnp.logical_not(lo_half)
        qs_ref[h * tq:(h + 1) * tq, :] = jnp.where(keep, blk, jnp.zeros_like(blk))
    wi = wi_ref[0]

    def chunk_keys(kchunk, valid):
        dots = _nt_dot(qs_ref[...], kchunk)
        s = jnp.zeros((tq, kc), F32)
        for h in range(N_IDX_HEADS):
            s = s + jnp.maximum(dots[h * tq:(h + 1) * tq], 0.0) * wi[:, h:h + 1]
        return jnp.where(valid, _sortable_key(s), INT_MIN)

    key_ref[0] = chunk_keys(kim_ref[...], col < n_meta)

    def score_body(c, carry):
        kchunk = ki_ref[0, pl.ds(pl.multiple_of(c * kc, kc), kc), :]
        key_ref[c + 1] = chunk_keys(kchunk, col + c * kc <= row + i * tq)
        return carry

    lax.fori_loop(0, i + 1, score_body, 0)

    nc = i + 2
    _select_topk(key_ref, bias_ref, nc, topk, tq, kc)

    for h in range(8):
        pr = slice((h // 2) * LANES, (h // 2 + 1) * LANES)
        keep = lo_half if h % 2 == 0 else jnp.logical_not(lo_half)
        qblk = q_ref[0, :, pr]
        qm = jnp.where(keep, qblk, jnp.zeros_like(qblk))

        def step(kchunk, vchunk, bias, m, l, acc):
            lg = _nt_dot(qm, kchunk) + bias
            m_new = jnp.maximum(m, jnp.max(lg, axis=1, keepdims=True))
            alpha = jnp.exp(m - m_new)
            p = jnp.exp(lg - m_new)
            l = alpha * l + jnp.sum(p, axis=1, keepdims=True)
            acc = alpha * acc + _dot(p.astype(BF16), vchunk)
            return m_new, l, acc

        m0 = jnp.full((tq, 1), NEG_BIG, F32)
        l0 = jnp.zeros((tq, 1), F32)
        a0 = jnp.zeros((tq, LANES), F32)
        carry = step(km_ref[:, pr], vm_ref[:, pr], bias_ref[0], m0, l0, a0)

        def attn_body(c, carry):
            rows = pl.ds(pl.multiple_of(c * kc, kc), kc)
            return step(kb_ref[0, rows, pr], vb_ref[0, rows, pr], bias_ref[c + 1], *carry)

        m, l, acc = lax.fori_loop(0, i + 1, attn_body, carry)
        out = acc / l
        if h % 2 == 0:
            out_even = out
        else:
            o_ref[0, :, pr] = jnp.where(lo_half, out_even, out).astype(BF16)


def _prompt_attention(pp, pm, *, tq, topk):
    B, T, _ = pp["q"].shape
    nq = T // tq
    nc1 = nq + 1
    tile = lambda w: pl.BlockSpec((1, tq, w), lambda b, i: (b, i, 0))
    whole = lambda w: pl.BlockSpec((1, T, w), lambda b, i: (b, 0, 0))
    meta = lambda w: pl.BlockSpec((tq, w), lambda b, i: (0, 0))
    return pl.pallas_call(
        functools.partial(_attn_kernel, topk=topk, n_meta=N_META),
        grid=(B, nq),
        in_specs=[tile(512), tile(512), tile(128), whole(512), whole(512), whole(128),
                  meta(512), meta(512), meta(128)],
        out_specs=tile(512),
        out_shape=jax.ShapeDtypeStruct((B, T, 512), BF16),
        scratch_shapes=[pltpu.VMEM((nc1, tq, tq), I32), pltpu.VMEM((nc1, tq, tq), F32),
                        pltpu.VMEM((N_IDX_HEADS * tq, LANES), BF16)],
        compiler_params=_cparams(("arbitrary", "arbitrary")),
        name="attn",
    )(pp["q"], pp["qi"], pp["wi"], pp["kb"], pp["vb"], pp["ki2"], pm["kb"], pm["vb"], pm["ki2"])


def _split_dot3(x, whi_ref, wlo_ref):
    xh = x.astype(BF16)
    xl = (x - xh.astype(F32)).astype(BF16)
    return _dot(xh, whi_ref[...]) + _dot(xl, whi_ref[...]) + _dot(xh, wlo_ref[...])


def _mix_router_kernel(yp_ref, o_ref, x_ref, wo_ref, g_ref, rwh_ref, rwl_ref, rb_ref,
                       x1_ref, xn_ref, gate_ref):
    mix = _dot(yp_ref[...], wo_ref[0:512, :]) + _dot(o_ref[...], wo_ref[512:1024, :])
    x1 = x_ref[...] + mix
    x1_ref[...] = x1
    ms = jnp.mean(x1 * x1, axis=-1, keepdims=True)
    xn = x1 * lax.rsqrt(ms + EPS) * g_ref[...]
    xn_ref[...] = xn.astype(BF16)

    lg = _split_dot3(xn, rwh_ref, rwl_ref) + rb_ref[...]
    tm = lg.shape[0]
    lane = lax.broadcasted_iota(I32, (tm, LANES), 1)
    ninf = -jnp.inf
    gl = jnp.where(lane < N_EXPERT_GROUPS, lg, ninf)
    gmax = jnp.max(gl, axis=1, keepdims=True)
    p_sel = 1.0 / jnp.sum(jnp.exp(gl - gmax), axis=1, keepdims=True)
    g_sel = jnp.min(jnp.where(gl == gmax, lane, LANES), axis=1, keepdims=True)
    e0 = N_EXPERT_GROUPS + g_sel * EXPERTS_PER_GROUP
    ev = jnp.where((lane >= e0) & (lane < e0 + EXPERTS_PER_GROUP), lg, ninf)
    m1 = jnp.max(ev, axis=1, keepdims=True)
    i1 = jnp.min(jnp.where(ev == m1, lane, LANES), axis=1, keepdims=True)
    ev2 = jnp.where(lane == i1, ninf, ev)
    m2 = jnp.max(ev2, axis=1, keepdims=True)
    i2 = jnp.min(jnp.where(ev2 == m2, lane, LANES), axis=1, keepdims=True)
    e2 = jnp.exp(m2 - m1)
    inv = p_sel / (1.0 + e2)
    gate_ref[...] = jnp.where(lane == i1, inv, 0.0) + jnp.where(lane == i2, e2 * inv, 0.0)


def _mix_router(yp, o, x, p, *, tm):
    N, D = x.shape
    row = lambda w: pl.BlockSpec((tm, w), lambda i: (i, 0))
    const = lambda a: pl.BlockSpec(a.shape, lambda i: (0,) * a.ndim)
    consts = [p["w_out"], p["ffn_g"], p["rw_hi"], p["rw_lo"], p["rb"]]
    return pl.pallas_call(
        _mix_router_kernel,
        grid=(N // tm,),
        in_specs=[row(512), row(512), row(D)] + [const(a) for a in consts],
        out_specs=[row(D), row(D), row(LANES)],
        out_shape=[jax.ShapeDtypeStruct((N, D), F32), jax.ShapeDtypeStruct((N, D), BF16),
                   jax.ShapeDtypeStruct((N, LANES), F32)],
        compiler_params=_cparams(("arbitrary",)),
        name="mix_router",
    )(yp, o, x, *consts)


def _moe_kernel(xn_ref, gate_ref, x1_ref, wg_ref, wu_ref, wd_ref, y_ref, acc_ref):
    e = pl.program_id(1)

    @pl.when(e == 0)
    def _():
        acc_ref[...] = jnp.zeros_like(acc_ref)

    xn = xn_ref[...]
    hg = _dot(xn, wg_ref[0])
    hu = _dot(xn, wu_ref[0])
    hact = (hg * (1.0 / (1.0 + jnp.exp(-hg))) * hu).astype(BF16)
    d = _dot(hact, wd_ref[0])
    lane = lax.broadcasted_iota(I32, gate_ref.shape, 1)
    gcol = jnp.sum(jnp.where(lane == e + N_EXPERT_GROUPS, gate_ref[...], 0.0), axis=1, keepdims=True)
    acc_ref[...] += gcol * d

    @pl.when(e == pl.num_programs(1) - 1)
    def _():
        y_ref[...] = x1_ref[...] + acc_ref[...]


def _moe(xn, gate, x1, p, *, tm):
    N, D = x1.shape
    E, _, FF = p["w_gate"].shape
    row = lambda w: pl.BlockSpec((tm, w), lambda i, e: (i, 0))
    return pl.pallas_call(
        _moe_kernel,
        grid=(N // tm, E),
        in_specs=[row(D), row(LANES), row(D),
                  pl.BlockSpec((1, D, FF), lambda i, e: (e, 0, 0)),
                  pl.BlockSpec((1, D, FF), lambda i, e: (e, 0, 0)),
                  pl.BlockSpec((1, FF, D), lambda i, e: (e, 0, 0))],
        out_specs=row(D),
        out_shape=jax.ShapeDtypeStruct((N, D), F32),
        scratch_shapes=[pltpu.VMEM((tm, D), F32)],
        compiler_params=_cparams(("arbitrary", "arbitrary")),
        name="moe",
    )(xn, gate, x1, p["w_gate"], p["w_up"], p["w_down"])


def _dec_pool_kernel(st_ref, u_ref, pw_ref, ps_ref, yp_ref):
    n_prev = st_ref.shape[0]
    u = u_ref[...]
    for g, w in enumerate(POOL_WINDOWS):
        cs = slice(g * LANES, (g + 1) * LANES)
        acc = u[:, cs]
        for j in range(1, w):
            acc = acc + st_ref[n_prev - j, :, cs]
        m = (acc * (1.0 / w) - u[:, cs]).astype(BF16)
        yp_ref[:, cs] = (_dot(m, pw_ref[g]) * ps_ref[:, cs]).astype(BF16)


def _dec_pool(state_t, u, p):
    return pl.pallas_call(
        _dec_pool_kernel,
        out_shape=jax.ShapeDtypeStruct(u.shape, BF16),
        compiler_params=pltpu.CompilerParams(vmem_limit_bytes=VMEM_LIMIT),
        name="dec_pool",
    )(state_t, u, p["pool_w"], p["pool_s"])


def _dec_scores_kernel(pt_ref, qi_ref, w_ref, kin_ref, kidx_hbm, out_ref, buf, sem, *, n_pages, page, kc):
    b = pl.program_id(0)
    nb = pl.num_programs(0)

    def copies(bb, slot):
        return [pltpu.make_async_copy(kidx_hbm.at[0, pt_ref[bb, pg]],
                                      buf.at[slot, pl.ds(pg * page, page), :], sem.at[slot])
                for pg in range(n_pages)]

    @pl.when(b == 0)
    def _():
        for c in copies(0, 0):
            c.start()

    @pl.when(b + 1 < nb)
    def _():
        for c in copies(b + 1, (b + 1) % 2):
            c.start()

    slot = b % 2
    for c in copies(b, slot):
        c.wait()

    qi = qi_ref[0]
    w = w_ref[0]
    past = n_pages * page
    dots = _nt_dot(qi, buf[slot].astype(BF16))
    s = jnp.sum(jnp.maximum(dots, 0.0) * w, axis=0, keepdims=True)
    for c in range(past // kc):
        out_ref[c, pl.ds(b, 1), :] = s[:, c * kc:(c + 1) * kc]
    d_new = jnp.sum(qi.astype(F32) * kin_ref[0].astype(F32), axis=1, keepdims=True)
    s_new = jnp.sum(jnp.maximum(d_new, 0.0) * w, axis=0, keepdims=True)
    lane = lax.broadcasted_iota(I32, (1, kc), 1)
    out_ref[past // kc, pl.ds(b, 1), :] = jnp.where(lane == 0, s_new, -jnp.inf)


def _dec_scores(page_table, qi3, w3, kin3, cache_kidx, *, kc):
    DB, n_pages = page_table.shape
    page = cache_kidx.shape[2]
    past = n_pages * page
    nc1 = past // kc + 1
    grid_spec = pltpu.PrefetchScalarGridSpec(
        num_scalar_prefetch=1,
        grid=(DB,),
        in_specs=[pl.BlockSpec((1, N_IDX_HEADS, IDX_DIM), lambda b, pt: (b, 0, 0)),
                  pl.BlockSpec((1, N_IDX_HEADS, 1), lambda b, pt: (b, 0, 0)),
                  pl.BlockSpec((1, 1, IDX_DIM), lambda b, pt: (b, 0, 0)),
                  pl.BlockSpec(memory_space=pl.ANY)],
        out_specs=pl.BlockSpec((nc1, DB, kc), lambda b, pt: (0, 0, 0)),
        scratch_shapes=[pltpu.VMEM((2, past, IDX_DIM), F32), pltpu.SemaphoreType.DMA((2,))],
    )
    return pl.pallas_call(
        functools.partial(_dec_scores_kernel, n_pages=n_pages, page=page, kc=kc),
        grid_spec=grid_spec,
        out_shape=jax.ShapeDtypeStruct((nc1, DB, kc), F32),
        compiler_params=_cparams(("arbitrary",)),
        name="dec_scores",
    )(page_table, qi3, w3, kin3, cache_kidx)


def _dec_topk_kernel(sc_ref, pos_ref, meta_ref, key_ref, bias_ref, rank_ref, *, topk, past):
    nc1, db, kc = sc_ref.shape
    lane = lax.broadcasted_iota(I32, (db, kc), 1)

    def key_body(c, carry):
        key_ref[c] = jnp.where(c * kc + lane <= past, _sortable_key(sc_ref[c]), INT_MIN)
        return carry

    lax.fori_loop(0, nc1, key_body, 0)
    _select_topk(key_ref, bias_ref, nc1, topk, db, kc)

    ra = lax.broadcasted_iota(I32, (kc, kc), 0)
    rb = lax.broadcasted_iota(I32, (kc, kc), 1)
    before = jnp.where(ra < rb, 1.0, 0.0).astype(BF16)

    def rank_body(c, off):
        sel = (bias_ref[c] == 0.0) & (c * kc + lane < past)
        ones = jnp.where(sel, 1.0, 0.0)
        rank_ref[c] = jnp.where(sel, off + _dot(ones.astype(BF16), before), -1.0)
        return off + jnp.sum(ones, axis=1, keepdims=True)

    nsel = lax.fori_loop(0, nc1, rank_body, jnp.zeros((db, 1), F32))
    cn, ln = past // kc, past % kc
    sel_new = bias_ref[cn][:, ln:ln + 1] == 0.0

    lane_k = lax.broadcasted_iota(I32, (db, topk), 1)

    def slot_body(r, out):
        rf = jnp.asarray(r, I32).astype(F32)

        def body(c, acc):
            return acc + jnp.where(rank_ref[c] == rf, (c * kc + lane).astype(F32), 0.0)

        acc = lax.fori_loop(0, nc1, body, jnp.zeros((db, kc), F32))
        return jnp.where(lane_k == r, jnp.sum(acc, axis=1, keepdims=True), out)

    out = lax.fori_loop(0, topk, slot_body, jnp.zeros((db, topk), F32))
    pos_ref[...] = out.astype(I32)
    lane_m = lax.broadcasted_iota(I32, meta_ref.shape, 1)
    meta_ref[...] = jnp.where(lane_m == 0, nsel.astype(I32),
                              jnp.where((lane_m == 1) & sel_new, 1, 0))


def _dec_topk(scores, *, topk, past):
    nc1, db, kc = scores.shape
    return pl.pallas_call(
        functools.partial(_dec_topk_kernel, topk=topk, past=past),
        out_shape=[jax.ShapeDtypeStruct((db, topk), I32), jax.ShapeDtypeStruct((db, LANES), I32)],
        scratch_shapes=[pltpu.VMEM((nc1, db, kc), I32), pltpu.VMEM((nc1, db, kc), F32),
                        pltpu.VMEM((nc1, db, kc), F32)],
        compiler_params=pltpu.CompilerParams(vmem_limit_bytes=VMEM_LIMIT),
        name="dec_topk",
    )(scores)


def _dec_attn_kernel(pt_ref, pos_ref, meta_ref, q_ref, kn_ref, vn_ref, ck_hbm, cv_hbm, o_ref,
                     kbuf, vbuf, sem, *, page):
    topk = kbuf.shape[0]
    shift = page.bit_length() - 1

    def copy_pair(r):
        pos = pos_ref[0, 0, r]
        row = pt_ref[0, 0, pos >> shift] * page + (pos & (page - 1))
        return (pltpu.make_async_copy(ck_hbm.at[pl.ds(row, 1), :], kbuf.at[pl.ds(r, 1), :], sem.at[0]),
                pltpu.make_async_copy(cv_hbm.at[pl.ds(row, 1), :], vbuf.at[pl.ds(r, 1), :], sem.at[1]))

    def start_body(r, carry):
        ck, cv = copy_pair(r)
        ck.start()
        cv.start()
        return carry

    lax.fori_loop(0, topk, start_body, 0, unroll=8)

    def wait_body(r, carry):
        ck, cv = copy_pair(r)
        ck.wait()
        cv.wait()
        return carry

    lax.fori_loop(0, topk, wait_body, 0, unroll=8)

    nsel = meta_ref[0, 0, 0]
    sel_new = meta_ref[0, 0, 1] > 0
    nh = 512 // HEAD_DIM
    q = q_ref[0]
    hrow = lax.broadcasted_iota(I32, (nh, 512), 0)
    hlane = lax.broadcasted_iota(I32, (nh, 512), 1) // HEAD_DIM
    own = hrow == hlane
    qh = jnp.where(own, q.astype(F32), 0.0)

    lg = jnp.zeros((nh, topk), F32)
    for pr in range(nh // 2):
        ls = slice(pr * LANES, (pr + 1) * LANES)
        lg = lg + _nt_dot(qh[:, ls].astype(BF16), kbuf[:, ls].astype(BF16))
    slot = lax.broadcasted_iota(I32, (nh, topk), 1)
    lg = jnp.where(slot < nsel, lg, NEG_BIG)
    l_new = jnp.sum(qh * kn_ref[0].astype(F32), axis=1, keepdims=True)
    l_new = jnp.where(sel_new, l_new, NEG_BIG)
    m = jnp.maximum(jnp.max(lg, axis=1, keepdims=True), l_new)
    p = jnp.exp(lg - m)
    p_new = jnp.exp(l_new - m)
    denom = jnp.sum(p, axis=1, keepdims=True) + p_new
    pb = p.astype(BF16)
    parts = [_dot(pb, vbuf[:, pr * LANES:(pr + 1) * LANES].astype(BF16)) for pr in range(nh // 2)]
    o8 = jnp.concatenate(parts, axis=1) + p_new * vn_ref[0].astype(F32)
    o8 = jnp.where(own, o8 / denom, 0.0)
    o_ref[0] = jnp.sum(o8, axis=0, keepdims=True).astype(BF16)


def _dec_attn(page_table3, pos3, meta3, q3, kn3, vn3, ck_rows, cv_rows, *, page):
    DB, _, topk = pos3.shape
    n_pages = page_table3.shape[2]
    smem = lambda w: pl.BlockSpec((1, 1, w), lambda b: (b, 0, 0), memory_space=pltpu.SMEM)
    vrow = lambda w: pl.BlockSpec((1, 1, w), lambda b: (b, 0, 0))
    return pl.pallas_call(
        functools.partial(_dec_attn_kernel, page=page),
        grid=(DB,),
        in_specs=[smem(n_pages), smem(topk), smem(LANES), vrow(512), vrow(512), vrow(512),
                  pl.BlockSpec(memory_space=pl.ANY), pl.BlockSpec(memory_space=pl.ANY)],
        out_specs=vrow(512),
        out_shape=jax.ShapeDtypeStruct((DB, 1, 512), BF16),
        scratch_shapes=[pltpu.VMEM((topk, 512), F32), pltpu.VMEM((topk, 512), F32),
                        pltpu.SemaphoreType.DMA((2,))],
        compiler_params=_cparams(("arbitrary",)),
        name="dec_attn",
    )(page_table3, pos3, meta3, q3, kn3, vn3, ck_rows, cv_rows)


def _prep_params(w_in, w_out, attn_norm, ffn_norm, q_norm, k_norm, kidx_norm, pool_w, pool_scale,
                 router_group_w, router_group_b, router_expert_w, router_expert_b,
                 expert_w_gate, expert_w_up, expert_w_down):
    d = w_in.shape[0]
    off_ki = 512 * 5
    w_main = w_in[:, :off_ki]
    w_ki = w_in[:, off_ki:off_ki + IDX_DIM]
    w_wi = w_in[:, off_ki + IDX_DIM:]
    w_cat = jnp.concatenate(
        [w_main, w_ki, w_ki, w_wi, jnp.zeros((d, LANES - w_wi.shape[1]), w_in.dtype)], axis=1)
    n_heads = 512 // HEAD_DIM
    gidx = jnp.arange(256) // HEAD_DIM
    gmat = (gidx[:, None] == gidx[None, :]).astype(F32) * (1.0 / HEAD_DIM)
    rw = jnp.concatenate([router_group_w, router_expert_w], axis=1)
    rw = jnp.pad(rw, ((0, 0), (0, LANES - rw.shape[1])))
    rw_hi = rw.astype(BF16)
    rb = jnp.concatenate([router_group_b, router_expert_b])
    rb = jnp.pad(rb, (0, LANES - rb.shape[0]))[None, :]
    return dict(
        w_in=w_cat.astype(BF16),
        attn_g=attn_norm[None, :],
        q_g=jnp.tile(q_norm, n_heads)[None, :],
        k_g=jnp.tile(k_norm, n_heads)[None, :],
        ki_g=jnp.tile(kidx_norm, 2)[None, :],
        gmat=gmat.astype(BF16),
        pool_w=pool_w.astype(BF16),
        pool_s=pool_scale[None, :],
        w_out=w_out.astype(BF16),
        ffn_g=ffn_norm[None, :],
        rw_hi=rw_hi,
        rw_lo=(rw - rw_hi.astype(F32)).astype(BF16),
        rb=rb,
        w_gate=expert_w_gate.astype(BF16),
        w_up=expert_w_up.astype(BF16),
        w_down=expert_w_down.astype(BF16),
    )


def _pad_rows(a, n):
    return jnp.pad(a, ((0, n - a.shape[0]), (0, 0)))


def kernel(x_prompt, x_sample, cache_k, cache_v, cache_kidx, state_pool, page_table, meta_tokens, w_in, w_out, attn_norm, ffn_norm, q_norm, k_norm, kidx_norm, pool_w, pool_scale, router_group_w, router_group_b, router_expert_w, router_expert_b, expert_w_gate, expert_w_up, expert_w_down):
    B, S, D = x_prompt.shape
    DB, DS, _ = x_sample.shape
    depth = w_in.shape[0]
    assert depth == 1 and DS == 1
    n_heads = 512 // HEAD_DIM
    p = _prep_params(w_in[0], w_out[0], attn_norm[0], ffn_norm[0], q_norm[0], k_norm[0], kidx_norm[0],
                     pool_w[0], pool_scale[0], router_group_w[0], router_group_b[0],
                     router_expert_w[0], router_expert_b[0], expert_w_gate[0], expert_w_up[0],
                     expert_w_down[0])
    tq = min(256, S)
    tm_proj = min(512, S)
    topk_p = min(TOPK_MAX, (S + N_META) // 4)

    zeros_hist = jnp.zeros((16, 512), F32)
    pm = _project(meta_tokens[None], p, zeros_hist, tm=N_META, with_pool=True)
    pp = _project(x_prompt, p, pm["ulast"][0], tm=tm_proj, with_pool=True)
    pmeta = {k: _pad_rows(pm[k][0], tq) for k in ("kb", "vb", "ki2")}
    o = _prompt_attention(pp, pmeta, tq=tq, topk=topk_p)
    n_tok = B * S
    tm_tok = min(512, n_tok)
    x1, xn2, gate = _mix_router(pp["yp"].reshape(n_tok, 512), o.reshape(n_tok, 512),
                                x_prompt.reshape(n_tok, D), p, tm=tm_tok)
    y_prompt = _moe(xn2, gate, x1, p, tm=tm_tok).reshape(B, S, D)

    def with_meta(a_meta, a):
        full = jnp.concatenate([jnp.broadcast_to(a_meta, (B,) + a_meta.shape[1:]), a], axis=1)
        return full

    new_k_prompt = with_meta(pm["kf"], pp["kf"]).reshape(1, B, S + N_META, n_heads, HEAD_DIM)
    new_v_prompt = with_meta(pm["vf"], pp["vf"]).reshape(1, B, S + N_META, n_heads, HEAD_DIM)
    new_kidx_prompt = with_meta(pm["kif"], pp["kif"])[None]
    new_pool_prompt = pp["ulast"][:, 1:, :][None]

    n_pages = page_table.shape[1]
    page = cache_k.shape[2]
    past = n_pages * page
    topk_s = min(TOPK_MAX, (past + DS) // 4)
    ps = _project(x_sample.reshape(1, DB, D), p, zeros_hist, tm=DB, with_pool=False)
    u_s = ps["yp"][0]
    yp_s = _dec_pool(jnp.swapaxes(state_pool[0], 0, 1), u_s, p)
    scores = _dec_scores(page_table, ps["qi"][0].reshape(DB, N_IDX_HEADS, IDX_DIM),
                         ps["wi"][0][:, :N_IDX_HEADS].reshape(DB, N_IDX_HEADS, 1),
                         ps["ki2"][0][:, :IDX_DIM].reshape(DB, 1, IDX_DIM), cache_kidx, kc=256)
    pos, meta = _dec_topk(scores, topk=topk_s, past=past)
    o_s = _dec_attn(page_table.reshape(DB, 1, n_pages), pos.reshape(DB, 1, topk_s),
                    meta.reshape(DB, 1, LANES), ps["q"][0].reshape(DB, 1, 512),
                    ps["kb"][0].reshape(DB, 1, 512), ps["vb"][0].reshape(DB, 1, 512),
                    cache_k[0].reshape(-1, 512), cache_v[0].reshape(-1, 512), page=page)
    x1_s, xn2_s, gate_s = _mix_router(yp_s, o_s.reshape(DB, 512), x_sample.reshape(DB, D), p, tm=DB)
    y_sample = _moe(xn2_s, gate_s, x1_s, p, tm=DB).reshape(DB, DS, D)
    new_k_sample = ps["kf"].reshape(1, DB, DS, n_heads, HEAD_DIM)
    new_v_sample = ps["vf"].reshape(1, DB, DS, n_heads, HEAD_DIM)
    new_kidx_sample = ps["kif"].reshape(1, DB, DS, IDX_DIM)
    new_pool_sample = jnp.concatenate([state_pool[0][:, 1:, :], u_s[:, None, :]], axis=1)[None]
    return (y_prompt, y_sample, new_k_prompt, new_v_prompt, new_kidx_prompt, new_pool_prompt,
            new_k_sample, new_v_sample, new_kidx_sample, new_pool_sample)
```

```python
import functools

import jax
import jax.numpy as jnp
from jax import lax
from jax.experimental import pallas as pl
from jax.experimental.pallas import tpu as pltpu

F32 = jnp.float32
BF16 = jnp.bfloat16
I32 = jnp.int32

N_META = 16
POOL_WINDOWS = (2, 4, 8, 16)
HEAD_DIM = 64
N_IDX_HEADS = 8
IDX_DIM = 64
TOPK_MAX = 256
N_EXPERT_GROUPS = 4
EXPERTS_PER_GROUP = 4
EPS = 1e-6

LANES = 128
LOG2E = 1.4426950408889634
V_ROWS = HEAD_DIM + 16
META_PAD = 128
F32_MAX = 3.4028234663852886e38
INT_MIN = -(2 ** 31)
NEG_BIG = -1e30
VMEM_LIMIT = 56 * 1024 * 1024


def _cparams(sem):
    return pltpu.CompilerParams(dimension_semantics=sem, vmem_limit_bytes=VMEM_LIMIT)


def _nt_dot(a, b):
    return lax.dot_general(a, b, (((1,), (1,)), ((), ())), preferred_element_type=F32)


def _dot(a, b):
    return jnp.dot(a, b, preferred_element_type=F32)


def _group_mean_sq(hx, g_ref):
    sq = hx * hx
    hi = sq.astype(BF16)
    lo = (sq - hi.astype(F32)).astype(BF16)
    g = g_ref[...]
    halves = []
    for j in range(2):
        sl = slice(j * 256, (j + 1) * 256)
        halves.append(_dot(hi[:, sl], g) + _dot(lo[:, sl], g))
    return jnp.concatenate(halves, axis=1)


def _proj_kernel(x_ref, g_ref, w_ref, qg_ref, kg_ref, kig_ref, gm_ref, pw_ref, ps_ref, uh_ref,
                 yp_ref, q_ref, kb_ref, vb_ref, qi_ref, ki2_ref, wi_ref, kf_ref, vf_ref, kif_ref, ul_ref,
                 uext_ref, *, with_pool):
    i = pl.program_id(1)
    tm = x_ref.shape[1]
    x = x_ref[0]
    ms = jnp.mean(x * x, axis=-1, keepdims=True)
    xn = (x * lax.rsqrt(ms + EPS) * g_ref[...]).astype(BF16)
    h = _dot(xn, w_ref[...])

    u = h[:, 0:512]
    if with_pool:
        @pl.when(i == 0)
        def _():
            uext_ref[0:16, :] = uh_ref[...]

        uext_ref[16:16 + tm, :] = u
        for g, w in enumerate(POOL_WINDOWS):
            cs = slice(g * LANES, (g + 1) * LANES)
            acc = u[:, cs]
            for j in range(1, w):
                acc = acc + uext_ref[16 - j:16 - j + tm, cs]
            m = (acc * (1.0 / w) - u[:, cs]).astype(BF16)
            y = _dot(m, pw_ref[g]) * ps_ref[:, cs]
            yp_ref[0, :, cs] = y.astype(BF16)
        ul_ref[0] = uext_ref[tm:tm + 16, :]
        uext_ref[0:16, :] = uext_ref[tm:tm + 16, :]
    else:
        yp_ref[0] = u
        ul_ref[0] = jnp.zeros(ul_ref.shape[1:], F32)

    hq = h[:, 512:1024]
    qn = hq * lax.rsqrt(_group_mean_sq(hq, gm_ref) + EPS) * qg_ref[...]
    hk = h[:, 1024:1536]
    kn = hk * lax.rsqrt(_group_mean_sq(hk, gm_ref) + EPS) * kg_ref[...]
    kf_ref[0] = kn
    kb_ref[0] = kn.astype(BF16)
    hv = h[:, 1536:2048]
    vf_ref[0] = hv
    qi = h[:, 2048:2560] * (IDX_DIM ** -0.5)
    hki = h[:, 2560:2688]
    kin = hki * lax.rsqrt(jnp.mean(hki * hki, axis=-1, keepdims=True) + EPS) * kig_ref[...]
    kif_ref[0] = kin[:, 0:IDX_DIM]
    wi = h[:, 2688:2816] * (N_IDX_HEADS ** -0.5)

    if with_pool:
        q_ref[0] = (qn * (HEAD_DIM ** -0.5 * LOG2E)).T.astype(BF16)
        qi_ref[0] = qi.T.astype(BF16)
        wi_ref[0] = wi.T[0:N_IDX_HEADS, :]
        ki2_ref[0] = kin[:, 0:IDX_DIM].astype(BF16)
        hvt = hv.T.astype(BF16)
        for hd in range(512 // HEAD_DIM):
            r0 = hd * V_ROWS
            vb_ref[0, r0:r0 + HEAD_DIM, :] = hvt[hd * HEAD_DIM:(hd + 1) * HEAD_DIM, :]
            vb_ref[0, r0 + HEAD_DIM:r0 + V_ROWS, :] = jnp.ones((V_ROWS - HEAD_DIM, tm), BF16)
    else:
        q_ref[0] = (qn * (HEAD_DIM ** -0.5)).astype(BF16)
        qi_ref[0] = qi.astype(BF16)
        wi_ref[0] = wi
        ki2_ref[0] = kin.astype(BF16)
        vb_ref[0] = hv.astype(BF16)


def _project(x, p, u_hist, *, tm, with_pool):
    B, T, D = x.shape
    nt = T // tm
    tile = lambda w: pl.BlockSpec((1, tm, w), lambda b, i: (b, i, 0))
    const2 = lambda a: pl.BlockSpec(a.shape, lambda b, i: (0,) * a.ndim)
    ins = [x, p["attn_g"], p["w_in"], p["q_g"], p["k_g"], p["ki_g"], p["gmat"], p["pool_w"], p["pool_s"], u_hist]
    in_specs = [tile(D)] + [const2(a) for a in ins[1:]]
    sds = jax.ShapeDtypeStruct
    ttile = lambda r: pl.BlockSpec((1, r, tm), lambda b, i: (b, 0, i))
    if with_pool:
        v_rows = (512 // HEAD_DIM) * V_ROWS
        out_shape = [
            sds((B, T, 512), BF16),
            sds((B, 512, T), BF16),
            sds((B, T, 512), BF16),
            sds((B, v_rows, T), BF16),
            sds((B, 512, T), BF16),
            sds((B, T, IDX_DIM), BF16),
            sds((B, N_IDX_HEADS, T), F32),
        ]
        out_specs = [tile(512), ttile(512), tile(512), ttile(v_rows), ttile(512), tile(IDX_DIM),
                     ttile(N_IDX_HEADS)]
    else:
        out_shape = [
            sds((B, T, 512), F32),
            sds((B, T, 512), BF16),
            sds((B, T, 512), BF16),
            sds((B, T, 512), BF16),
            sds((B, T, 512), BF16),
            sds((B, T, 128), BF16),
            sds((B, T, 128), F32),
        ]
        out_specs = [tile(512)] * 5 + [tile(128), tile(128)]
    out_shape += [sds((B, T, 512), F32), sds((B, T, 512), F32), sds((B, T, IDX_DIM), F32),
                  sds((B, 16, 512), F32)]
    out_specs += [tile(512), tile(512), tile(IDX_DIM), pl.BlockSpec((1, 16, 512), lambda b, i: (b, 0, 0))]
    names = ["yp", "q", "kb", "vb", "qi", "ki2", "wi", "kf", "vf", "kif", "ulast"]
    outs = pl.pallas_call(
        functools.partial(_proj_kernel, with_pool=with_pool),
        grid=(B, nt),
        in_specs=in_specs,
        out_specs=out_specs,
        out_shape=out_shape,
        scratch_shapes=[pltpu.VMEM((tm + 16, 512), F32)],
        compiler_params=_cparams(("arbitrary", "arbitrary")),
        name="proj",
    )(*ins)
    return dict(zip(names, outs))


def _sortable_key(s):
    bits = pltpu.bitcast(s, I32)
    bits = jnp.where(bits == INT_MIN, 0, bits)
    return bits ^ ((bits >> 31) & 0x7FFFFFFF)


def _select_topk(key_ref, bias_ref, nc, topk, tq, kc):
    def count(pred):
        def body(c, acc):
            return acc + jnp.where(pred(c, key_ref[c]), 1.0, 0.0)

        acc = lax.fori_loop(0, nc, body, jnp.zeros((tq, kc), F32))
        return jnp.sum(acc, axis=1, keepdims=True)

    kf = float(topk)

    def bit_body(j, t_u):
        cand_u = t_u | jnp.left_shift(jnp.int32(1), 31 - j)
        cand_s = cand_u ^ INT_MIN
        cnt = count(lambda c, k: k >= cand_s)
        return jnp.where(cnt >= kf, cand_u, t_u)

    t_u = lax.fori_loop(0, 32, bit_body, jnp.zeros((tq, 1), I32))
    t = t_u ^ INT_MIN
    cnt_ge = count(lambda c, k: k >= t)
    has_tie = jnp.max(jnp.where(t > INT_MIN, cnt_ge, 0.0)) > kf

    def write_bias(sel_fn):
        def body(c, carry):
            k = key_ref[c]
            bias_ref[c] = jnp.where(sel_fn(c, k) & (k > INT_MIN), 0.0, NEG_BIG)
            return carry

        lax.fori_loop(0, nc, body, 0)

    @pl.when(jnp.logical_not(has_tie))
    def _():
        write_bias(lambda c, k: k >= t)

    @pl.when(has_tie)
    def _():
        lane = lax.broadcasted_iota(I32, (tq, kc), 1)
        need = kf - count(lambda c, k: k > t)

        def pos_body(j, p):
            cand = p | jnp.left_shift(jnp.int32(1), 15 - j)
            cnt = count(lambda c, k: (k == t) & (c * kc + lane < cand))
            return jnp.where(cnt <= need, cand, p)

        p = lax.fori_loop(0, 16, pos_body, jnp.zeros((tq, 1), I32))
        write_bias(lambda c, k: (k > t) | ((k == t) & (c * kc + lane < p)))


def _key_to_f32(ku):
    ks = ku ^ INT_MIN
    return lax.bitcast_convert_type(ks ^ ((ks >> 31) & 0x7FFFFFFF), F32)


def _attn_kernel(qT_ref, qiT_ref, wiT_ref, kb_ref, vT_ref, ki_ref, kbm_ref, vTm_ref, kim_ref, o_ref,
                 sc_ref, scm_ref, bias_ref, biasm_ref, qm_ref, lg_ref, p_ref, acc_ref, m_ref, l_ref, a_ref,
                 *, topk, n_meta):
    i = pl.program_id(1)
    tq = qT_ref.shape[2]
    kc = tq
    mp = kbm_ref.shape[0]
    nh = 512 // HEAD_DIM
    nc = i + 1
    kf = float(topk)
    ninf = -jnp.inf
    wiT = wiT_ref[0]
    rowm = lax.broadcasted_iota(I32, (mp, tq), 0)
    rowc = lax.broadcasted_iota(I32, (kc, tq), 0)
    colc = lax.broadcasted_iota(I32, (kc, tq), 1)

    def rows_of(c):
        return pl.ds(pl.multiple_of(c * kc, kc), kc)

    def scores(ki_chunk):
        s = None
        for h in range(N_IDX_HEADS):
            d = _dot(ki_chunk, qiT_ref[0, h * IDX_DIM:(h + 1) * IDX_DIM, :])
            t = jnp.maximum(d, 0.0) * wiT[h:h + 1, :]
            s = t if s is None else s + t
        return s

    scm_ref[...] = jnp.where(rowm >= mp - n_meta, scores(kim_ref[...]), ninf)

    def score_body(c, carry):
        sc_ref[c] = scores(ki_ref[0, rows_of(c), :])
        return carry

    lax.fori_loop(0, i, score_body, 0)
    sc_ref[i] = jnp.where(rowc <= colc, scores(ki_ref[0, rows_of(i), :]), ninf)

    def count(pred):
        acc0 = jnp.sum(jnp.where(pred(scm_ref[...], 0), 1.0, 0.0).reshape(mp // 8, 8, tq), axis=0)

        def body(c, acc):
            hit = jnp.where(pred(sc_ref[c], mp + c * kc), 1.0, 0.0)
            return acc + jnp.sum(hit.reshape(kc // 8, 8, tq), axis=0)

        return jnp.sum(lax.fori_loop(0, nc, body, acc0), axis=0, keepdims=True)

    def bit_body(j, t_u):
        cand_u = t_u | jnp.left_shift(jnp.int32(1), 31 - j)
        cand = _key_to_f32(cand_u)
        cnt = count(lambda s, p0: s >= cand)
        return jnp.where(cnt >= kf, cand_u, t_u)

    t_u = lax.fori_loop(0, 32, bit_body, jnp.zeros((1, tq), I32))
    lane1 = lax.broadcasted_iota(I32, (1, tq), 1)
    n_valid = (i * tq + lane1 + (1 + n_meta)).astype(F32)
    t = jnp.where(n_valid > kf, _key_to_f32(t_u), -F32_MAX)
    cnt_ge = count(lambda s, p0: s >= t)
    has_tie = jnp.max(jnp.where(n_valid > kf, cnt_ge, 0.0)) > kf

    def write_bias(sel):
        biasm_ref[...] = jnp.where(sel(scm_ref[...], 0), 0.0, NEG_BIG)

        def body(c, carry):
            bias_ref[c] = jnp.where(sel(sc_ref[c], mp + c * kc), 0.0, NEG_BIG)
            return carry

        lax.fori_loop(0, nc, body, 0)

    @pl.when(jnp.logical_not(has_tie))
    def _():
        write_bias(lambda s, p0: s >= t)

    @pl.when(has_tie)
    def _():
        need = kf - count(lambda s, p0: s > t)

        def pos_of(s, p0):
            return p0 + lax.broadcasted_iota(I32, s.shape, 0)

        def pos_body(j, p):
            cand = p | jnp.left_shift(jnp.int32(1), 15 - j)
            cnt = count(lambda s, p0: (s == t) & (pos_of(s, p0) < cand))
            return jnp.where(cnt <= need, cand, p)

        p = lax.fori_loop(0, 16, pos_body, jnp.zeros((1, tq), I32))
        write_bias(lambda s, p0: (s > t) | ((s == t) & (pos_of(s, p0) < p)))

    sub = lax.broadcasted_iota(I32, (LANES, tq), 0)
    for h in range(nh):
        blk = qT_ref[0, (h // 2) * LANES:(h // 2 + 1) * LANES, :]
        keep = (sub < HEAD_DIM) if h % 2 == 0 else (sub >= HEAD_DIM)
        qm_ref[h] = jnp.where(keep, blk, jnp.zeros_like(blk))
    m_ref[...] = jnp.full(m_ref.shape, NEG_BIG, F32)
    l_ref[...] = jnp.zeros(l_ref.shape, F32)
    acc_ref[...] = jnp.zeros(acc_ref.shape, F32)

    def attend(k_of, v_of, bias, n):
        for h in range(nh):
            pr = slice((h // 2) * LANES, (h // 2 + 1) * LANES)
            lg = _dot(k_of(pr), qm_ref[h]) + bias
            lg_ref[h, 0:n, :] = lg
            m_old = m_ref[h:h + 1, :]
            m_new = jnp.maximum(m_old, jnp.max(lg, axis=0, keepdims=True))
            a_ref[h:h + 1, :] = jnp.exp2(m_old - m_new)
            m_ref[h:h + 1, :] = m_new
        for h in range(nh):
            p_ref[h, 0:n, :] = jnp.exp2(lg_ref[h, 0:n, :] - m_ref[h:h + 1, :]).astype(BF16)
        for h in range(nh):
            pv = _dot(v_of(slice(h * V_ROWS, (h + 1) * V_ROWS)), p_ref[h, 0:n, :])
            hs = slice(h * HEAD_DIM, (h + 1) * HEAD_DIM)
            alpha = a_ref[h:h + 1, :]
            acc_ref[hs, :] = alpha * acc_ref[hs, :] + pv[0:HEAD_DIM, :]
            l_ref[h:h + 1, :] = alpha * l_ref[h:h + 1, :] + pv[HEAD_DIM:HEAD_DIM + 1, :]

    attend(lambda pr: kbm_ref[:, pr], lambda vs: vTm_ref[vs, :], biasm_ref[...], mp)

    def attn_body(c, carry):
        rows = rows_of(c)
        attend(lambda pr: kb_ref[0, rows, pr], lambda vs: vT_ref[0, vs, rows], bias_ref[c], kc)
        return carry

    lax.fori_loop(0, nc, attn_body, 0)

    for h in range(nh):
        hs = slice(h * HEAD_DIM, (h + 1) * HEAD_DIM)
        acc_ref[hs, :] = acc_ref[hs, :] / l_ref[h:h + 1, :]
    o_ref[0] = acc_ref[...].T.astype(BF16)


def _prompt_attention(pp, pm, *, tq, topk):
    B, T, _ = pp["kb"].shape
    nq = T // tq
    nh = 512 // HEAD_DIM
    mp = pm["kb"].shape[0]
    ttile = lambda r: pl.BlockSpec((1, r, tq), lambda b, i: (b, 0, i))
    whole = lambda a: pl.BlockSpec((1,) + a.shape[1:], lambda b, i: (b, 0, 0))
    const = lambda a: pl.BlockSpec(a.shape, lambda b, i: (0, 0))
    return pl.pallas_call(
        functools.partial(_attn_kernel, topk=topk, n_meta=N_META),
        grid=(B, nq),
        in_specs=[ttile(512), ttile(512), ttile(N_IDX_HEADS), whole(pp["kb"]), whole(pp["vb"]),
                  whole(pp["ki2"]), const(pm["kb"]), const(pm["vb"]), const(pm["ki2"])],
        out_specs=pl.BlockSpec((1, tq, 512), lambda b, i: (b, i, 0)),
        out_shape=jax.ShapeDtypeStruct((B, T, 512), BF16),
        scratch_shapes=[pltpu.VMEM((nq, tq, tq), F32), pltpu.VMEM((mp, tq), F32),
                        pltpu.VMEM((nq, tq, tq), F32), pltpu.VMEM((mp, tq), F32),
                        pltpu.VMEM((nh, LANES, tq), BF16),
                        pltpu.VMEM((nh, tq, tq), F32), pltpu.VMEM((nh, tq, tq), BF16),
                        pltpu.VMEM((512, tq), F32),
                        pltpu.VMEM((nh, tq), F32), pltpu.VMEM((nh, tq), F32), pltpu.VMEM((nh, tq), F32)],
        compiler_params=_cparams(("arbitrary", "arbitrary")),
        name="attn",
    )(pp["q"], pp["qi"], pp["wi"], pp["kb"], pp["vb"], pp["ki2"], pm["kb"], pm["vb"], pm["ki2"])


def _split_dot3(x, whi_ref, wlo_ref):
    xh = x.astype(BF16)
    xl = (x - xh.astype(F32)).astype(BF16)
    return _dot(xh, whi_ref[...]) + _dot(xl, whi_ref[...]) + _dot(xh, wlo_ref[...])


def _mix_router_kernel(yp_ref, o_ref, x_ref, wo_ref, g_ref, rwh_ref, rwl_ref, rb_ref,
                       x1_ref, xn_ref, gate_ref):
    mix = _dot(yp_ref[...], wo_ref[0:512, :]) + _dot(o_ref[...], wo_ref[512:1024, :])
    x1 = x_ref[...] + mix
    x1_ref[...] = x1
    ms = jnp.mean(x1 * x1, axis=-1, keepdims=True)
    xn = x1 * lax.rsqrt(ms + EPS) * g_ref[...]
    xn_ref[...] = xn.astype(BF16)

    lg = _split_dot3(xn, rwh_ref, rwl_ref) + rb_ref[...]
    tm = lg.shape[0]
    lane = lax.broadcasted_iota(I32, (tm, LANES), 1)
    ninf = -jnp.inf
    gl = jnp.where(lane < N_EXPERT_GROUPS, lg, ninf)
    gmax = jnp.max(gl, axis=1, keepdims=True)
    p_sel = 1.0 / jnp.sum(jnp.exp(gl - gmax), axis=1, keepdims=True)
    g_sel = jnp.min(jnp.where(gl == gmax, lane, LANES), axis=1, keepdims=True)
    e0 = N_EXPERT_GROUPS + g_sel * EXPERTS_PER_GROUP
    ev = jnp.where((lane >= e0) & (lane < e0 + EXPERTS_PER_GROUP), lg, ninf)
    m1 = jnp.max(ev, axis=1, keepdims=True)
    i1 = jnp.min(jnp.where(ev == m1, lane, LANES), axis=1, keepdims=True)
    ev2 = jnp.where(lane == i1, ninf, ev)
    m2 = jnp.max(ev2, axis=1, keepdims=True)
    i2 = jnp.min(jnp.where(ev2 == m2, lane, LANES), axis=1, keepdims=True)
    e2 = jnp.exp(m2 - m1)
    inv = p_sel / (1.0 + e2)
    gate_ref[...] = jnp.where(lane == i1, inv, 0.0) + jnp.where(lane == i2, e2 * inv, 0.0)


def _mix_router(yp, o, x, p, *, tm):
    N, D = x.shape
    row = lambda w: pl.BlockSpec((tm, w), lambda i: (i, 0))
    const = lambda a: pl.BlockSpec(a.shape, lambda i: (0,) * a.ndim)
    consts = [p["w_out"], p["ffn_g"], p["rw_hi"], p["rw_lo"], p["rb"]]
    return pl.pallas_call(
        _mix_router_kernel,
        grid=(N // tm,),
        in_specs=[row(512), row(512), row(D)] + [const(a) for a in consts],
        out_specs=[row(D), row(D), row(LANES)],
        out_shape=[jax.ShapeDtypeStruct((N, D), F32), jax.ShapeDtypeStruct((N, D), BF16),
                   jax.ShapeDtypeStruct((N, LANES), F32)],
        compiler_params=_cparams(("arbitrary",)),
        name="mix_router",
    )(yp, o, x, *consts)


def _moe_kernel(xn_ref, gate_ref, x1_ref, wg_ref, wu_ref, wd_ref, y_ref, acc_ref):
    e = pl.program_id(1)

    @pl.when(e == 0)
    def _():
        acc_ref[...] = jnp.zeros_like(acc_ref)

    xn = xn_ref[...]
    hg = _dot(xn, wg_ref[0])
    hu = _dot(xn, wu_ref[0])
    hact = (hg * (1.0 / (1.0 + jnp.exp(-hg))) * hu).astype(BF16)
    d = _dot(hact, wd_ref[0])
    lane = lax.broadcasted_iota(I32, gate_ref.shape, 1)
    gcol = jnp.sum(jnp.where(lane == e + N_EXPERT_GROUPS, gate_ref[...], 0.0), axis=1, keepdims=True)
    acc_ref[...] += gcol * d

    @pl.when(e == pl.num_programs(1) - 1)
    def _():
        y_ref[...] = x1_ref[...] + acc_ref[...]


def _moe(xn, gate, x1, p, *, tm):
    N, D = x1.shape
    E, _, FF = p["w_gate"].shape
    row = lambda w: pl.BlockSpec((tm, w), lambda i, e: (i, 0))
    return pl.pallas_call(
        _moe_kernel,
        grid=(N // tm, E),
        in_specs=[row(D), row(LANES), row(D),
                  pl.BlockSpec((1, D, FF), lambda i, e: (e, 0, 0)),
                  pl.BlockSpec((1, D, FF), lambda i, e: (e, 0, 0)),
                  pl.BlockSpec((1, FF, D), lambda i, e: (e, 0, 0))],
        out_specs=row(D),
        out_shape=jax.ShapeDtypeStruct((N, D), F32),
        scratch_shapes=[pltpu.VMEM((tm, D), F32)],
        compiler_params=_cparams(("arbitrary", "arbitrary")),
        name="moe",
    )(xn, gate, x1, p["w_gate"], p["w_up"], p["w_down"])


def _dec_pool_kernel(st_ref, u_ref, pw_ref, ps_ref, yp_ref):
    n_prev = st_ref.shape[0]
    u = u_ref[...]
    for g, w in enumerate(POOL_WINDOWS):
        cs = slice(g * LANES, (g + 1) * LANES)
        acc = u[:, cs]
        for j in range(1, w):
            acc = acc + st_ref[n_prev - j, :, cs]
        m = (acc * (1.0 / w) - u[:, cs]).astype(BF16)
        yp_ref[:, cs] = (_dot(m, pw_ref[g]) * ps_ref[:, cs]).astype(BF16)


def _dec_pool(state_t, u, p):
    return pl.pallas_call(
        _dec_pool_kernel,
        out_shape=jax.ShapeDtypeStruct(u.shape, BF16),
        compiler_params=pltpu.CompilerParams(vmem_limit_bytes=VMEM_LIMIT),
        name="dec_pool",
    )(state_t, u, p["pool_w"], p["pool_s"])


def _dec_scores_kernel(pt_ref, qi_ref, w_ref, kin_ref, kidx_hbm, out_ref, buf, sem, *, n_pages, page, kc):
    b = pl.program_id(0)
    nb = pl.num_programs(0)

    def copies(bb, slot):
        return [pltpu.make_async_copy(kidx_hbm.at[0, pt_ref[bb, pg]],
                                      buf.at[slot, pl.ds(pg * page, page), :], sem.at[slot])
                for pg in range(n_pages)]

    @pl.when(b == 0)
    def _():
        for c in copies(0, 0):
            c.start()

    @pl.when(b + 1 < nb)
    def _():
        for c in copies(b + 1, (b + 1) % 2):
            c.start()

    slot = b % 2
    for c in copies(b, slot):
        c.wait()

    qi = qi_ref[0]
    w = w_ref[0]
    past = n_pages * page
    dots = _nt_dot(qi, buf[slot].astype(BF16))
    s = jnp.sum(jnp.maximum(dots, 0.0) * w, axis=0, keepdims=True)
    for c in range(past // kc):
        out_ref[c, pl.ds(b, 1), :] = s[:, c * kc:(c + 1) * kc]
    d_new = jnp.sum(qi.astype(F32) * kin_ref[0].astype(F32), axis=1, keepdims=True)
    s_new = jnp.sum(jnp.maximum(d_new, 0.0) * w, axis=0, keepdims=True)
    lane = lax.broadcasted_iota(I32, (1, kc), 1)
    out_ref[past // kc, pl.ds(b, 1), :] = jnp.where(lane == 0, s_new, -jnp.inf)


def _dec_scores(page_table, qi3, w3, kin3, cache_kidx, *, kc):
    DB, n_pages = page_table.shape
    page = cache_kidx.shape[2]
    past = n_pages * page
    nc1 = past // kc + 1
    grid_spec = pltpu.PrefetchScalarGridSpec(
        num_scalar_prefetch=1,
        grid=(DB,),
        in_specs=[pl.BlockSpec((1, N_IDX_HEADS, IDX_DIM), lambda b, pt: (b, 0, 0)),
                  pl.BlockSpec((1, N_IDX_HEADS, 1), lambda b, pt: (b, 0, 0)),
                  pl.BlockSpec((1, 1, IDX_DIM), lambda b, pt: (b, 0, 0)),
                  pl.BlockSpec(memory_space=pl.ANY)],
        out_specs=pl.BlockSpec((nc1, DB, kc), lambda b, pt: (0, 0, 0)),
        scratch_shapes=[pltpu.VMEM((2, past, IDX_DIM), F32), pltpu.SemaphoreType.DMA((2,))],
    )
    return pl.pallas_call(
        functools.partial(_dec_scores_kernel, n_pages=n_pages, page=page, kc=kc),
        grid_spec=grid_spec,
        out_shape=jax.ShapeDtypeStruct((nc1, DB, kc), F32),
        compiler_params=_cparams(("arbitrary",)),
        name="dec_scores",
    )(page_table, qi3, w3, kin3, cache_kidx)


def _dec_topk_kernel(sc_ref, pos_ref, meta_ref, key_ref, bias_ref, rank_ref, *, topk, past):
    nc1, db, kc = sc_ref.shape
    lane = lax.broadcasted_iota(I32, (db, kc), 1)

    def key_body(c, carry):
        key_ref[c] = jnp.where(c * kc + lane <= past, _sortable_key(sc_ref[c]), INT_MIN)
        return carry

    lax.fori_loop(0, nc1, key_body, 0)
    _select_topk(key_ref, bias_ref, nc1, topk, db, kc)

    ra = lax.broadcasted_iota(I32, (kc, kc), 0)
    rb = lax.broadcasted_iota(I32, (kc, kc), 1)
    before = jnp.where(ra < rb, 1.0, 0.0).astype(BF16)

    def rank_body(c, off):
        sel = (bias_ref[c] == 0.0) & (c * kc + lane < past)
        ones = jnp.where(sel, 1.0, 0.0)
        rank_ref[c] = jnp.where(sel, off + _dot(ones.astype(BF16), before), -1.0)
        return off + jnp.sum(ones, axis=1, keepdims=True)

    nsel = lax.fori_loop(0, nc1, rank_body, jnp.zeros((db, 1), F32))
    cn, ln = past // kc, past % kc
    sel_new = bias_ref[cn][:, ln:ln + 1] == 0.0

    lane_k = lax.broadcasted_iota(I32, (db, topk), 1)

    def slot_body(r, out):
        rf = jnp.asarray(r, I32).astype(F32)

        def body(c, acc):
            return acc + jnp.where(rank_ref[c] == rf, (c * kc + lane).astype(F32), 0.0)

        acc = lax.fori_loop(0, nc1, body, jnp.zeros((db, kc), F32))
        return jnp.where(lane_k == r, jnp.sum(acc, axis=1, keepdims=True), out)

    out = lax.fori_loop(0, topk, slot_body, jnp.zeros((db, topk), F32))
    pos_ref[...] = out.astype(I32)
    lane_m = lax.broadcasted_iota(I32, meta_ref.shape, 1)
    meta_ref[...] = jnp.where(lane_m == 0, nsel.astype(I32),
                              jnp.where((lane_m == 1) & sel_new, 1, 0))


def _dec_topk(scores, *, topk, past):
    nc1, db, kc = scores.shape
    return pl.pallas_call(
        functools.partial(_dec_topk_kernel, topk=topk, past=past),
        out_shape=[jax.ShapeDtypeStruct((db, topk), I32), jax.ShapeDtypeStruct((db, LANES), I32)],
        scratch_shapes=[pltpu.VMEM((nc1, db, kc), I32), pltpu.VMEM((nc1, db, kc), F32),
                        pltpu.VMEM((nc1, db, kc), F32)],
        compiler_params=pltpu.CompilerParams(vmem_limit_bytes=VMEM_LIMIT),
        name="dec_topk",
    )(scores)


def _dec_attn_kernel(pt_ref, pos_ref, meta_ref, q_ref, kn_ref, vn_ref, ck_hbm, cv_hbm, o_ref,
                     kbuf, vbuf, sem, *, page):
    topk = kbuf.shape[0]
    shift = page.bit_length() - 1

    def copy_pair(r):
        pos = pos_ref[0, 0, r]
        pg = pt_ref[0, 0, pos >> shift]
        off = pos & (page - 1)
        return (pltpu.make_async_copy(ck_hbm.at[0, pg, off], kbuf.at[r], sem.at[0]),
                pltpu.make_async_copy(cv_hbm.at[0, pg, off], vbuf.at[r], sem.at[1]))

    def start_body(r, carry):
        ck, cv = copy_pair(r)
        ck.start()
        cv.start()
        return carry

    lax.fori_loop(0, topk, start_body, 0, unroll=8)

    def wait_body(r, carry):
        ck, cv = copy_pair(r)
        ck.wait()
        cv.wait()
        return carry

    lax.fori_loop(0, topk, wait_body, 0, unroll=8)

    nsel = meta_ref[0, 0, 0]
    sel_new = meta_ref[0, 0, 1] > 0
    q = q_ref[0].astype(F32)
    kr = kbuf[...].astype(BF16).astype(F32)
    vr = vbuf[...].astype(BF16).astype(F32)
    lg = jnp.sum(kr * q[None], axis=-1, keepdims=True)
    slot = lax.broadcasted_iota(I32, lg.shape, 0)
    lg = jnp.where(slot < nsel, lg, NEG_BIG)
    l_new = jnp.sum(kn_ref[0].astype(F32) * q, axis=-1, keepdims=True)
    l_new = jnp.where(sel_new, l_new, NEG_BIG)
    m = jnp.maximum(jnp.max(lg, axis=0), l_new)
    p = jnp.exp(lg - m[None])
    p_new = jnp.exp(l_new - m)
    denom = jnp.sum(p, axis=0) + p_new
    out = jnp.sum(p * vr, axis=0) + p_new * vn_ref[0].astype(F32)
    o_ref[0] = (out / denom).astype(BF16)


def _dec_attn(page_table3, pos3, meta3, q3, kn3, vn3, cache_k, cache_v):
    DB, _, topk = pos3.shape
    n_pages = page_table3.shape[2]
    _, _, page, nh, hd = cache_k.shape
    smem = lambda w: pl.BlockSpec((1, 1, w), lambda b: (b, 0, 0), memory_space=pltpu.SMEM)
    head = pl.BlockSpec((1, nh, hd), lambda b: (b, 0, 0))
    return pl.pallas_call(
        functools.partial(_dec_attn_kernel, page=page),
        grid=(DB,),
        in_specs=[smem(n_pages), smem(topk), smem(LANES), head, head, head,
                  pl.BlockSpec(memory_space=pl.ANY), pl.BlockSpec(memory_space=pl.ANY)],
        out_specs=head,
        out_shape=jax.ShapeDtypeStruct((DB, nh, hd), BF16),
        scratch_shapes=[pltpu.VMEM((topk, nh, hd), F32), pltpu.VMEM((topk, nh, hd), F32),
                        pltpu.SemaphoreType.DMA((2,))],
        compiler_params=_cparams(("arbitrary",)),
        name="dec_attn",
    )(page_table3, pos3, meta3, q3, kn3, vn3, cache_k, cache_v)


def _prep_params(w_in, w_out, attn_norm, ffn_norm, q_norm, k_norm, kidx_norm, pool_w, pool_scale,
                 router_group_w, router_group_b, router_expert_w, router_expert_b,
                 expert_w_gate, expert_w_up, expert_w_down):
    d = w_in.shape[0]
    off_ki = 512 * 5
    w_main = w_in[:, :off_ki]
    w_ki = w_in[:, off_ki:off_ki + IDX_DIM]
    w_wi = w_in[:, off_ki + IDX_DIM:]
    w_cat = jnp.concatenate(
        [w_main, w_ki, w_ki, w_wi, jnp.zeros((d, LANES - w_wi.shape[1]), w_in.dtype)], axis=1)
    n_heads = 512 // HEAD_DIM
    gidx = jnp.arange(256) // HEAD_DIM
    gmat = (gidx[:, None] == gidx[None, :]).astype(F32) * (1.0 / HEAD_DIM)
    rw = jnp.concatenate([router_group_w, router_expert_w], axis=1)
    rw = jnp.pad(rw, ((0, 0), (0, LANES - rw.shape[1])))
    rw_hi = rw.astype(BF16)
    rb = jnp.concatenate([router_group_b, router_expert_b])
    rb = jnp.pad(rb, (0, LANES - rb.shape[0]))[None, :]
    return dict(
        w_in=w_cat.astype(BF16),
        attn_g=attn_norm[None, :],
        q_g=jnp.tile(q_norm, n_heads)[None, :],
        k_g=jnp.tile(k_norm, n_heads)[None, :],
        ki_g=jnp.tile(kidx_norm, 2)[None, :],
        gmat=gmat.astype(BF16),
        pool_w=pool_w.astype(BF16),
        pool_s=pool_scale[None, :],
        w_out=w_out.astype(BF16),
        ffn_g=ffn_norm[None, :],
        rw_hi=rw_hi,
        rw_lo=(rw - rw_hi.astype(F32)).astype(BF16),
        rb=rb,
        w_gate=expert_w_gate.astype(BF16),
        w_up=expert_w_up.astype(BF16),
        w_down=expert_w_down.astype(BF16),
    )


def _pad_rows(a, n):
    return jnp.pad(a, ((0, n - a.shape[0]), (0, 0)))


def kernel(x_prompt, x_sample, cache_k, cache_v, cache_kidx, state_pool, page_table, meta_tokens, w_in, w_out, attn_norm, ffn_norm, q_norm, k_norm, kidx_norm, pool_w, pool_scale, router_group_w, router_group_b, router_expert_w, router_expert_b, expert_w_gate, expert_w_up, expert_w_down):
    B, S, D = x_prompt.shape
    DB, DS, _ = x_sample.shape
    depth = w_in.shape[0]
    assert depth == 1 and DS == 1
    n_heads = 512 // HEAD_DIM
    p = _prep_params(w_in[0], w_out[0], attn_norm[0], ffn_norm[0], q_norm[0], k_norm[0], kidx_norm[0],
                     pool_w[0], pool_scale[0], router_group_w[0], router_group_b[0],
                     router_expert_w[0], router_expert_b[0], expert_w_gate[0], expert_w_up[0],
                     expert_w_down[0])
    tq = min(256, S)
    tm_proj = min(512, S)
    topk_p = min(TOPK_MAX, (S + N_META) // 4)

    zeros_hist = jnp.zeros((16, 512), F32)
    x_meta = jnp.concatenate([jnp.zeros((META_PAD - N_META, D), F32), meta_tokens.astype(F32)], axis=0)
    pm = _project(x_meta[None], p, zeros_hist, tm=META_PAD, with_pool=True)
    pp = _project(x_prompt, p, pm["ulast"][0], tm=tm_proj, with_pool=True)
    pmeta = {k: pm[k][0] for k in ("kb", "vb", "ki2")}
    o = _prompt_attention(pp, pmeta, tq=tq, topk=topk_p)
    n_tok = B * S
    tm_tok = min(512, n_tok)
    x1, xn2, gate = _mix_router(pp["yp"].reshape(n_tok, 512), o.reshape(n_tok, 512),
                                x_prompt.reshape(n_tok, D), p, tm=tm_tok)
    y_prompt = _moe(xn2, gate, x1, p, tm=tm_tok).reshape(B, S, D)

    def with_meta(a_meta, a):
        a_meta = a_meta[:, META_PAD - N_META:]
        full = jnp.concatenate([jnp.broadcast_to(a_meta, (B,) + a_meta.shape[1:]), a], axis=1)
        return full

    new_k_prompt = with_meta(pm["kf"], pp["kf"]).reshape(1, B, S + N_META, n_heads, HEAD_DIM)
    new_v_prompt = with_meta(pm["vf"], pp["vf"]).reshape(1, B, S + N_META, n_heads, HEAD_DIM)
    new_kidx_prompt = with_meta(pm["kif"], pp["kif"])[None]
    new_pool_prompt = pp["ulast"][:, 1:, :][None]

    n_pages = page_table.shape[1]
    page = cache_k.shape[2]
    past = n_pages * page
    topk_s = min(TOPK_MAX, (past + DS) // 4)
    ps = _project(x_sample.reshape(1, DB, D), p, zeros_hist, tm=DB, with_pool=False)
    u_s = ps["yp"][0]
    yp_s = _dec_pool(jnp.swapaxes(state_pool[0], 0, 1), u_s, p)
    scores = _dec_scores(page_table, ps["qi"][0].reshape(DB, N_IDX_HEADS, IDX_DIM),
                         ps["wi"][0][:, :N_IDX_HEADS].reshape(DB, N_IDX_HEADS, 1),
                         ps["ki2"][0][:, :IDX_DIM].reshape(DB, 1, IDX_DIM), cache_kidx, kc=256)
    pos, meta = _dec_topk(scores, topk=topk_s, past=past)
    o_s = _dec_attn(page_table.reshape(DB, 1, n_pages), pos.reshape(DB, 1, topk_s),
                    meta.reshape(DB, 1, LANES), ps["q"][0].reshape(DB, n_heads, HEAD_DIM),
                    ps["kb"][0].reshape(DB, n_heads, HEAD_DIM), ps["vb"][0].reshape(DB, n_heads, HEAD_DIM),
                    cache_k, cache_v)
    x1_s, xn2_s, gate_s = _mix_router(yp_s, o_s.reshape(DB, 512), x_sample.reshape(DB, D), p, tm=DB)
    y_sample = _moe(xn2_s, gate_s, x1_s, p, tm=DB).reshape(DB, DS, D)
    new_k_sample = ps["kf"].reshape(1, DB, DS, n_heads, HEAD_DIM)
    new_v_sample = ps["vf"].reshape(1, DB, DS, n_heads, HEAD_DIM)
    new_kidx_sample = ps["kif"].reshape(1, DB, DS, IDX_DIM)
    new_pool_sample = jnp.concatenate([state_pool[0][:, 1:, :], u_s[:, None, :]], axis=1)[None]
    return (y_prompt, y_sample, new_k_prompt, new_v_prompt, new_kidx_prompt, new_pool_prompt,
            new_k_sample, new_v_sample, new_kidx_sample, new_pool_sample)
```

```python
import functools

import jax
import jax.numpy as jnp
from jax import lax
from jax.experimental import pallas as pl
from jax.experimental.pallas import tpu as pltpu

F32 = jnp.float32
BF16 = jnp.bfloat16
I32 = jnp.int32

N_META = 16
POOL_WINDOWS = (2, 4, 8, 16)
HEAD_DIM = 64
N_IDX_HEADS = 8
IDX_DIM = 64
TOPK_MAX = 256
N_EXPERT_GROUPS = 4
EXPERTS_PER_GROUP = 4
EPS = 1e-6

LANES = 128
LOG2E = 1.4426950408889634
V_ROWS = HEAD_DIM + 16
META_PAD = 128
F32_MAX = 3.4028234663852886e38
DEC_RING = 8
INT_MIN = -(2 ** 31)
NEG_BIG = -1e30
VMEM_LIMIT = 56 * 1024 * 1024


def _cparams(sem):
    return pltpu.CompilerParams(dimension_semantics=sem, vmem_limit_bytes=VMEM_LIMIT)


def _nt_dot(a, b):
    return lax.dot_general(a, b, (((1,), (1,)), ((), ())), preferred_element_type=F32)


def _dot(a, b):
    return jnp.dot(a, b, preferred_element_type=F32)


def _group_mean_sq(hx, g_ref):
    sq = hx * hx
    hi = sq.astype(BF16)
    lo = (sq - hi.astype(F32)).astype(BF16)
    g = g_ref[...]
    halves = []
    for j in range(2):
        sl = slice(j * 256, (j + 1) * 256)
        halves.append(_dot(hi[:, sl], g) + _dot(lo[:, sl], g))
    return jnp.concatenate(halves, axis=1)


def _proj_kernel(x_ref, g_ref, w_ref, qg_ref, kg_ref, kig_ref, gm_ref, pw_ref, ps_ref, uh_ref,
                 yp_ref, q_ref, kb_ref, vb_ref, qi_ref, ki2_ref, wi_ref, kf_ref, vf_ref, kif_ref, ul_ref,
                 uext_ref, *, with_pool):
    i = pl.program_id(1)
    tm = x_ref.shape[1]
    x = x_ref[0]
    ms = jnp.mean(x * x, axis=-1, keepdims=True)
    xn = (x * lax.rsqrt(ms + EPS) * g_ref[...]).astype(BF16)
    h = _dot(xn, w_ref[...])

    u = h[:, 0:512]
    if with_pool:
        @pl.when(i == 0)
        def _():
            uext_ref[0:16, :] = uh_ref[...]

        uext_ref[16:16 + tm, :] = u
        for g, w in enumerate(POOL_WINDOWS):
            cs = slice(g * LANES, (g + 1) * LANES)
            acc = u[:, cs]
            for j in range(1, w):
                acc = acc + uext_ref[16 - j:16 - j + tm, cs]
            m = (acc * (1.0 / w) - u[:, cs]).astype(BF16)
            y = _dot(m, pw_ref[g]) * ps_ref[:, cs]
            yp_ref[0, :, cs] = y.astype(BF16)
        ul_ref[0] = uext_ref[tm:tm + 16, :]
        uext_ref[0:16, :] = uext_ref[tm:tm + 16, :]
    else:
        yp_ref[0] = u
        ul_ref[0] = jnp.zeros(ul_ref.shape[1:], F32)

    hq = h[:, 512:1024]
    qn = hq * lax.rsqrt(_group_mean_sq(hq, gm_ref) + EPS) * qg_ref[...]
    hk = h[:, 1024:1536]
    kn = hk * lax.rsqrt(_group_mean_sq(hk, gm_ref) + EPS) * kg_ref[...]
    kf_ref[0] = kn
    kb_ref[0] = kn.astype(BF16)
    hv = h[:, 1536:2048]
    vf_ref[0] = hv
    qi = h[:, 2048:2560] * (IDX_DIM ** -0.5)
    hki = h[:, 2560:2688]
    kin = hki * lax.rsqrt(jnp.mean(hki * hki, axis=-1, keepdims=True) + EPS) * kig_ref[...]
    kif_ref[0] = kin[:, 0:IDX_DIM]
    wi = h[:, 2688:2816] * (N_IDX_HEADS ** -0.5)

    if with_pool:
        q_ref[0] = (qn * (HEAD_DIM ** -0.5 * LOG2E)).T.astype(BF16)
        qi_ref[0] = qi.T.astype(BF16)
        wi_ref[0] = wi.T[0:N_IDX_HEADS, :]
        ki2_ref[0] = kin[:, 0:IDX_DIM].astype(BF16)
        hvt = hv.T.astype(BF16)
        for hd in range(512 // HEAD_DIM):
            r0 = hd * V_ROWS
            vb_ref[0, r0:r0 + HEAD_DIM, :] = hvt[hd * HEAD_DIM:(hd + 1) * HEAD_DIM, :]
            vb_ref[0, r0 + HEAD_DIM:r0 + V_ROWS, :] = jnp.ones((V_ROWS - HEAD_DIM, tm), BF16)
    else:
        q_ref[0] = (qn * (HEAD_DIM ** -0.5)).astype(BF16)
        qi_ref[0] = qi.astype(BF16)
        wi_ref[0] = wi
        ki2_ref[0] = kin.astype(BF16)
        vb_ref[0] = hv.astype(BF16)


def _project(x, p, u_hist, *, tm, with_pool):
    B, T, D = x.shape
    nt = T // tm
    tile = lambda w: pl.BlockSpec((1, tm, w), lambda b, i: (b, i, 0))
    const2 = lambda a: pl.BlockSpec(a.shape, lambda b, i: (0,) * a.ndim)
    ins = [x, p["attn_g"], p["w_in"], p["q_g"], p["k_g"], p["ki_g"], p["gmat"], p["pool_w"], p["pool_s"], u_hist]
    in_specs = [tile(D)] + [const2(a) for a in ins[1:]]
    sds = jax.ShapeDtypeStruct
    ttile = lambda r: pl.BlockSpec((1, r, tm), lambda b, i: (b, 0, i))
    if with_pool:
        v_rows = (512 // HEAD_DIM) * V_ROWS
        out_shape = [
            sds((B, T, 512), BF16),
            sds((B, 512, T), BF16),
            sds((B, T, 512), BF16),
            sds((B, v_rows, T), BF16),
            sds((B, 512, T), BF16),
            sds((B, T, IDX_DIM), BF16),
            sds((B, N_IDX_HEADS, T), F32),
        ]
        out_specs = [tile(512), ttile(512), tile(512), ttile(v_rows), ttile(512), tile(IDX_DIM),
                     ttile(N_IDX_HEADS)]
    else:
        out_shape = [
            sds((B, T, 512), F32),
            sds((B, T, 512), BF16),
            sds((B, T, 512), BF16),
            sds((B, T, 512), BF16),
            sds((B, T, 512), BF16),
            sds((B, T, 128), BF16),
            sds((B, T, 128), F32),
        ]
        out_specs = [tile(512)] * 5 + [tile(128), tile(128)]
    out_shape += [sds((B, T, 512), F32), sds((B, T, 512), F32), sds((B, T, IDX_DIM), F32),
                  sds((B, 16, 512), F32)]
    out_specs += [tile(512), tile(512), tile(IDX_DIM), pl.BlockSpec((1, 16, 512), lambda b, i: (b, 0, 0))]
    names = ["yp", "q", "kb", "vb", "qi", "ki2", "wi", "kf", "vf", "kif", "ulast"]
    outs = pl.pallas_call(
        functools.partial(_proj_kernel, with_pool=with_pool),
        grid=(B, nt),
        in_specs=in_specs,
        out_specs=out_specs,
        out_shape=out_shape,
        scratch_shapes=[pltpu.VMEM((tm + 16, 512), F32)],
        compiler_params=_cparams(("arbitrary", "arbitrary")),
        name="proj",
    )(*ins)
    return dict(zip(names, outs))


def _sortable_key(s):
    bits = pltpu.bitcast(s, I32)
    bits = jnp.where(bits == INT_MIN, 0, bits)
    return bits ^ ((bits >> 31) & 0x7FFFFFFF)


def _select_topk(key_ref, bias_ref, nc, topk, tq, kc):
    def count(pred):
        def body(c, acc):
            return acc + jnp.where(pred(c, key_ref[c]), 1.0, 0.0)

        acc = lax.fori_loop(0, nc, body, jnp.zeros((tq, kc), F32))
        return jnp.sum(acc, axis=1, keepdims=True)

    kf = float(topk)

    def bit_body(j, t_u):
        cand_u = t_u | jnp.left_shift(jnp.int32(1), 31 - j)
        cand_s = cand_u ^ INT_MIN
        cnt = count(lambda c, k: k >= cand_s)
        return jnp.where(cnt >= kf, cand_u, t_u)

    t_u = lax.fori_loop(0, 32, bit_body, jnp.zeros((tq, 1), I32))
    t = t_u ^ INT_MIN
    cnt_ge = count(lambda c, k: k >= t)
    has_tie = jnp.max(jnp.where(t > INT_MIN, cnt_ge, 0.0)) > kf

    def write_bias(sel_fn):
        def body(c, carry):
            k = key_ref[c]
            bias_ref[c] = jnp.where(sel_fn(c, k) & (k > INT_MIN), 0.0, NEG_BIG)
            return carry

        lax.fori_loop(0, nc, body, 0)

    @pl.when(jnp.logical_not(has_tie))
    def _():
        write_bias(lambda c, k: k >= t)

    @pl.when(has_tie)
    def _():
        lane = lax.broadcasted_iota(I32, (tq, kc), 1)
        need = kf - count(lambda c, k: k > t)

        def pos_body(j, p):
            cand = p | jnp.left_shift(jnp.int32(1), 15 - j)
            cnt = count(lambda c, k: (k == t) & (c * kc + lane < cand))
            return jnp.where(cnt <= need, cand, p)

        p = lax.fori_loop(0, 16, pos_body, jnp.zeros((tq, 1), I32))
        write_bias(lambda c, k: (k > t) | ((k == t) & (c * kc + lane < p)))


def _key_to_f32(ku):
    ks = ku ^ INT_MIN
    return lax.bitcast_convert_type(ks ^ ((ks >> 31) & 0x7FFFFFFF), F32)


def _attn_kernel(qT_ref, qiT_ref, wiT_ref, kb_ref, vT_ref, ki_ref, kbm_ref, vTm_ref, kim_ref, o_ref,
                 sc_ref, scm_ref, bias_ref, biasm_ref, qm_ref, lg_ref, p_ref, acc_ref, m_ref, l_ref, a_ref,
                 *, topk, n_meta):
    i = pl.program_id(1)
    tq = qT_ref.shape[2]
    kc = tq
    mp = kbm_ref.shape[0]
    nh = 512 // HEAD_DIM
    nc = i + 1
    kf = float(topk)
    ninf = -jnp.inf
    wiT = wiT_ref[0]
    rowm = lax.broadcasted_iota(I32, (mp, tq), 0)
    rowc = lax.broadcasted_iota(I32, (kc, tq), 0)
    colc = lax.broadcasted_iota(I32, (kc, tq), 1)

    def rows_of(c):
        return pl.ds(pl.multiple_of(c * kc, kc), kc)

    def scores(ki_chunk):
        s = None
        for h in range(N_IDX_HEADS):
            d = _dot(ki_chunk, qiT_ref[0, h * IDX_DIM:(h + 1) * IDX_DIM, :])
            t = jnp.maximum(d, 0.0) * wiT[h:h + 1, :]
            s = t if s is None else s + t
        return s

    scm_ref[...] = jnp.where(rowm >= mp - n_meta, scores(kim_ref[...]), ninf)

    def score_body(c, carry):
        sc_ref[c] = scores(ki_ref[0, rows_of(c), :])
        return carry

    lax.fori_loop(0, i, score_body, 0)
    sc_ref[i] = jnp.where(rowc <= colc, scores(ki_ref[0, rows_of(i), :]), ninf)
    sc_ref[nc] = jnp.full((kc, tq), ninf, F32)

    half = (nc + 1) // 2

    def count(pred):
        def hits(s, p0):
            return jnp.sum(jnp.where(pred(s, p0), 1.0, 0.0).reshape(s.shape[0] // 8, 8, tq), axis=0)

        def body(c, acc):
            c2 = c + half
            return acc + hits(sc_ref[c], mp + c * kc) + hits(sc_ref[c2], mp + c2 * kc)

        return jnp.sum(lax.fori_loop(0, half, body, hits(scm_ref[...], 0)), axis=0, keepdims=True)

    def bit_body(j, t_u):
        cand_u = t_u | jnp.left_shift(jnp.int32(1), 31 - j)
        cand = _key_to_f32(cand_u)
        cnt = count(lambda s, p0: s >= cand)
        return jnp.where(cnt >= kf, cand_u, t_u)

    t_u = lax.fori_loop(0, 32, bit_body, jnp.zeros((1, tq), I32))
    lane1 = lax.broadcasted_iota(I32, (1, tq), 1)
    n_valid = (i * tq + lane1 + (1 + n_meta)).astype(F32)
    t = jnp.where(n_valid > kf, _key_to_f32(t_u), -F32_MAX)
    cnt_ge = count(lambda s, p0: s >= t)
    has_tie = jnp.max(jnp.where(n_valid > kf, cnt_ge, 0.0)) > kf

    def write_bias(sel):
        biasm_ref[...] = jnp.where(sel(scm_ref[...], 0), 0.0, NEG_BIG)

        def body(c, carry):
            bias_ref[c] = jnp.where(sel(sc_ref[c], mp + c * kc), 0.0, NEG_BIG)
            return carry

        lax.fori_loop(0, nc, body, 0)

    @pl.when(jnp.logical_not(has_tie))
    def _():
        write_bias(lambda s, p0: s >= t)

    @pl.when(has_tie)
    def _():
        need = kf - count(lambda s, p0: s > t)

        def pos_of(s, p0):
            return p0 + lax.broadcasted_iota(I32, s.shape, 0)

        def pos_body(j, p):
            cand = p | jnp.left_shift(jnp.int32(1), 15 - j)
            cnt = count(lambda s, p0: (s == t) & (pos_of(s, p0) < cand))
            return jnp.where(cnt <= need, cand, p)

        p = lax.fori_loop(0, 16, pos_body, jnp.zeros((1, tq), I32))
        write_bias(lambda s, p0: (s > t) | ((s == t) & (pos_of(s, p0) < p)))

    sub = lax.broadcasted_iota(I32, (LANES, tq), 0)
    for h in range(nh):
        blk = qT_ref[0, (h // 2) * LANES:(h // 2 + 1) * LANES, :]
        keep = (sub < HEAD_DIM) if h % 2 == 0 else (sub >= HEAD_DIM)
        qm_ref[h] = jnp.where(keep, blk, jnp.zeros_like(blk))
    m_ref[...] = jnp.full(m_ref.shape, NEG_BIG, F32)
    l_ref[...] = jnp.zeros(l_ref.shape, F32)
    acc_ref[...] = jnp.zeros(acc_ref.shape, F32)

    def attend(k_of, v_of, bias, n):
        for h in range(nh):
            pr = slice((h // 2) * LANES, (h // 2 + 1) * LANES)
            lg = _dot(k_of(pr), qm_ref[h]) + bias
            lg_ref[h, 0:n, :] = lg
            m_old = m_ref[h:h + 1, :]
            m_new = jnp.maximum(m_old, jnp.max(lg, axis=0, keepdims=True))
            a_ref[h:h + 1, :] = jnp.exp2(m_old - m_new)
            m_ref[h:h + 1, :] = m_new
        for h in range(nh):
            p_ref[h, 0:n, :] = jnp.exp2(lg_ref[h, 0:n, :] - m_ref[h:h + 1, :]).astype(BF16)
        for h in range(nh):
            pv = _dot(v_of(slice(h * V_ROWS, (h + 1) * V_ROWS)), p_ref[h, 0:n, :])
            hs = slice(h * HEAD_DIM, (h + 1) * HEAD_DIM)
            alpha = a_ref[h:h + 1, :]
            acc_ref[hs, :] = alpha * acc_ref[hs, :] + pv[0:HEAD_DIM, :]
            l_ref[h:h + 1, :] = alpha * l_ref[h:h + 1, :] + pv[HEAD_DIM:HEAD_DIM + 1, :]

    attend(lambda pr: kbm_ref[:, pr], lambda vs: vTm_ref[vs, :], biasm_ref[...], mp)

    def attn_body(c, carry):
        rows = rows_of(c)
        attend(lambda pr: kb_ref[0, rows, pr], lambda vs: vT_ref[0, vs, rows], bias_ref[c], kc)
        return carry

    lax.fori_loop(0, nc, attn_body, 0)

    for h in range(nh):
        hs = slice(h * HEAD_DIM, (h + 1) * HEAD_DIM)
        acc_ref[hs, :] = acc_ref[hs, :] / l_ref[h:h + 1, :]
    o_ref[0] = acc_ref[...].T.astype(BF16)


def _prompt_attention(pp, pm, *, tq, topk):
    B, T, _ = pp["kb"].shape
    nq = T // tq
    nh = 512 // HEAD_DIM
    mp = pm["kb"].shape[0]
    ttile = lambda r: pl.BlockSpec((1, r, tq), lambda b, i: (b, 0, i))
    whole = lambda a: pl.BlockSpec((1,) + a.shape[1:], lambda b, i: (b, 0, 0))
    const = lambda a: pl.BlockSpec(a.shape, lambda b, i: (0, 0))
    return pl.pallas_call(
        functools.partial(_attn_kernel, topk=topk, n_meta=N_META),
        grid=(B, nq),
        in_specs=[ttile(512), ttile(512), ttile(N_IDX_HEADS), whole(pp["kb"]), whole(pp["vb"]),
                  whole(pp["ki2"]), const(pm["kb"]), const(pm["vb"]), const(pm["ki2"])],
        out_specs=pl.BlockSpec((1, tq, 512), lambda b, i: (b, i, 0)),
        out_shape=jax.ShapeDtypeStruct((B, T, 512), BF16),
        scratch_shapes=[pltpu.VMEM((nq + 1, tq, tq), F32), pltpu.VMEM((mp, tq), F32),
                        pltpu.VMEM((nq, tq, tq), F32), pltpu.VMEM((mp, tq), F32),
                        pltpu.VMEM((nh, LANES, tq), BF16),
                        pltpu.VMEM((nh, tq, tq), F32), pltpu.VMEM((nh, tq, tq), BF16),
                        pltpu.VMEM((512, tq), F32),
                        pltpu.VMEM((nh, tq), F32), pltpu.VMEM((nh, tq), F32), pltpu.VMEM((nh, tq), F32)],
        compiler_params=_cparams(("arbitrary", "arbitrary")),
        name="attn",
    )(pp["q"], pp["qi"], pp["wi"], pp["kb"], pp["vb"], pp["ki2"], pm["kb"], pm["vb"], pm["ki2"])


def _split_dot3(x, whi_ref, wlo_ref):
    xh = x.astype(BF16)
    xl = (x - xh.astype(F32)).astype(BF16)
    return _dot(xh, whi_ref[...]) + _dot(xl, whi_ref[...]) + _dot(xh, wlo_ref[...])


def _mix_router_kernel(yp_ref, o_ref, x_ref, wo_ref, g_ref, rwh_ref, rwl_ref, rb_ref,
                       x1_ref, xn_ref, gate_ref):
    mix = _dot(yp_ref[...], wo_ref[0:512, :]) + _dot(o_ref[...], wo_ref[512:1024, :])
    x1 = x_ref[...] + mix
    x1_ref[...] = x1
    ms = jnp.mean(x1 * x1, axis=-1, keepdims=True)
    xn = x1 * lax.rsqrt(ms + EPS) * g_ref[...]
    xn_ref[...] = xn.astype(BF16)

    lg = _split_dot3(xn, rwh_ref, rwl_ref) + rb_ref[...]
    tm = lg.shape[0]
    lane = lax.broadcasted_iota(I32, (tm, LANES), 1)
    ninf = -jnp.inf
    gl = jnp.where(lane < N_EXPERT_GROUPS, lg, ninf)
    gmax = jnp.max(gl, axis=1, keepdims=True)
    p_sel = 1.0 / jnp.sum(jnp.exp(gl - gmax), axis=1, keepdims=True)
    g_sel = jnp.min(jnp.where(gl == gmax, lane, LANES), axis=1, keepdims=True)
    e0 = N_EXPERT_GROUPS + g_sel * EXPERTS_PER_GROUP
    ev = jnp.where((lane >= e0) & (lane < e0 + EXPERTS_PER_GROUP), lg, ninf)
    m1 = jnp.max(ev, axis=1, keepdims=True)
    i1 = jnp.min(jnp.where(ev == m1, lane, LANES), axis=1, keepdims=True)
    ev2 = jnp.where(lane == i1, ninf, ev)
    m2 = jnp.max(ev2, axis=1, keepdims=True)
    i2 = jnp.min(jnp.where(ev2 == m2, lane, LANES), axis=1, keepdims=True)
    e2 = jnp.exp(m2 - m1)
    inv = p_sel / (1.0 + e2)
    gate_ref[...] = jnp.where(lane == i1, inv, 0.0) + jnp.where(lane == i2, e2 * inv, 0.0)


def _mix_router(yp, o, x, p, *, tm):
    N, D = x.shape
    row = lambda w: pl.BlockSpec((tm, w), lambda i: (i, 0))
    const = lambda a: pl.BlockSpec(a.shape, lambda i: (0,) * a.ndim)
    consts = [p["w_out"], p["ffn_g"], p["rw_hi"], p["rw_lo"], p["rb"]]
    return pl.pallas_call(
        _mix_router_kernel,
        grid=(N // tm,),
        in_specs=[row(512), row(512), row(D)] + [const(a) for a in consts],
        out_specs=[row(D), row(D), row(LANES)],
        out_shape=[jax.ShapeDtypeStruct((N, D), F32), jax.ShapeDtypeStruct((N, D), BF16),
                   jax.ShapeDtypeStruct((N, LANES), F32)],
        compiler_params=_cparams(("arbitrary",)),
        name="mix_router",
    )(yp, o, x, *consts)


def _moe_kernel(xn_ref, gate_ref, x1_ref, wg_ref, wu_ref, wd_ref, y_ref, acc_ref):
    e = pl.program_id(1)

    @pl.when(e == 0)
    def _():
        acc_ref[...] = jnp.zeros_like(acc_ref)

    xn = xn_ref[...]
    hg = _dot(xn, wg_ref[0])
    hu = _dot(xn, wu_ref[0])
    hact = (hg * (1.0 / (1.0 + jnp.exp(-hg))) * hu).astype(BF16)
    d = _dot(hact, wd_ref[0])
    lane = lax.broadcasted_iota(I32, gate_ref.shape, 1)
    gcol = jnp.sum(jnp.where(lane == e + N_EXPERT_GROUPS, gate_ref[...], 0.0), axis=1, keepdims=True)
    acc_ref[...] += gcol * d

    @pl.when(e == pl.num_programs(1) - 1)
    def _():
        y_ref[...] = x1_ref[...] + acc_ref[...]


def _moe(xn, gate, x1, p, *, tm):
    N, D = x1.shape
    E, _, FF = p["w_gate"].shape
    row = lambda w: pl.BlockSpec((tm, w), lambda i, e: (i, 0))
    return pl.pallas_call(
        _moe_kernel,
        grid=(N // tm, E),
        in_specs=[row(D), row(LANES), row(D),
                  pl.BlockSpec((1, D, FF), lambda i, e: (e, 0, 0)),
                  pl.BlockSpec((1, D, FF), lambda i, e: (e, 0, 0)),
                  pl.BlockSpec((1, FF, D), lambda i, e: (e, 0, 0))],
        out_specs=row(D),
        out_shape=jax.ShapeDtypeStruct((N, D), F32),
        scratch_shapes=[pltpu.VMEM((tm, D), F32)],
        compiler_params=_cparams(("arbitrary", "arbitrary")),
        name="moe",
    )(xn, gate, x1, p["w_gate"], p["w_up"], p["w_down"])


def _dec_pool_kernel(st_ref, u_ref, pw_ref, ps_ref, yp_ref):
    n_prev = st_ref.shape[0]
    u = u_ref[...]
    for g, w in enumerate(POOL_WINDOWS):
        cs = slice(g * LANES, (g + 1) * LANES)
        acc = u[:, cs]
        for j in range(1, w):
            acc = acc + st_ref[n_prev - j, :, cs]
        m = (acc * (1.0 / w) - u[:, cs]).astype(BF16)
        yp_ref[:, cs] = (_dot(m, pw_ref[g]) * ps_ref[:, cs]).astype(BF16)


def _dec_pool(state_t, u, p):
    return pl.pallas_call(
        _dec_pool_kernel,
        out_shape=jax.ShapeDtypeStruct(u.shape, BF16),
        compiler_params=pltpu.CompilerParams(vmem_limit_bytes=VMEM_LIMIT),
        name="dec_pool",
    )(state_t, u, p["pool_w"], p["pool_s"])


def _dec_scores_kernel(pt_ref, qi_ref, w_ref, kin_ref, kidx_hbm, out_ref, buf, sem, *, n_pages, page, kc):
    b = pl.program_id(0)
    nb = pl.num_programs(0)

    def copies(bb, slot):
        return [pltpu.make_async_copy(kidx_hbm.at[0, pt_ref[bb, pg]],
                                      buf.at[slot, :, pl.ds(pg * page, page)], sem.at[slot])
                for pg in range(n_pages)]

    @pl.when(b == 0)
    def _():
        for c in copies(0, 0):
            c.start()

    @pl.when(b + 1 < nb)
    def _():
        for c in copies(b + 1, (b + 1) % 2):
            c.start()

    slot = b % 2
    for c in copies(b, slot):
        c.wait()

    qi = qi_ref[0]
    w = w_ref[0]
    past = n_pages * page
    dots = _dot(qi, buf[slot].astype(BF16))
    s = jnp.sum(jnp.maximum(dots, 0.0) * w, axis=0, keepdims=True)
    for c in range(past // kc):
        out_ref[c, pl.ds(b, 1), :] = s[:, c * kc:(c + 1) * kc]
    d_new = jnp.sum(qi.astype(F32) * kin_ref[0].astype(F32), axis=1, keepdims=True)
    s_new = jnp.sum(jnp.maximum(d_new, 0.0) * w, axis=0, keepdims=True)
    lane = lax.broadcasted_iota(I32, (1, kc), 1)
    out_ref[past // kc, pl.ds(b, 1), :] = jnp.where(lane == 0, s_new, -jnp.inf)


def _dec_scores(page_table, qi3, w3, kin3, cache_kidx_t, *, kc):
    DB, n_pages = page_table.shape
    page = cache_kidx_t.shape[3]
    past = n_pages * page
    nc1 = past // kc + 1
    grid_spec = pltpu.PrefetchScalarGridSpec(
        num_scalar_prefetch=1,
        grid=(DB,),
        in_specs=[pl.BlockSpec((1, N_IDX_HEADS, IDX_DIM), lambda b, pt: (b, 0, 0)),
                  pl.BlockSpec((1, N_IDX_HEADS, 1), lambda b, pt: (b, 0, 0)),
                  pl.BlockSpec((1, 1, IDX_DIM), lambda b, pt: (b, 0, 0)),
                  pl.BlockSpec(memory_space=pl.ANY)],
        out_specs=pl.BlockSpec((nc1, DB, kc), lambda b, pt: (0, 0, 0)),
        scratch_shapes=[pltpu.VMEM((2, IDX_DIM, past), F32), pltpu.SemaphoreType.DMA((2,))],
    )
    return pl.pallas_call(
        functools.partial(_dec_scores_kernel, n_pages=n_pages, page=page, kc=kc),
        grid_spec=grid_spec,
        out_shape=jax.ShapeDtypeStruct((nc1, DB, kc), F32),
        compiler_params=_cparams(("arbitrary",)),
        name="dec_scores",
    )(page_table, qi3, w3, kin3, cache_kidx_t)


def _dec_topk_kernel(sc_ref, bias_ref, key_ref, *, topk, past):
    nc1, db, kc = sc_ref.shape
    lane = lax.broadcasted_iota(I32, (db, kc), 1)

    def key_body(c, carry):
        key_ref[c] = jnp.where(c * kc + lane <= past, _sortable_key(sc_ref[c]), INT_MIN)
        return carry

    lax.fori_loop(0, nc1, key_body, 0)
    _select_topk(key_ref, bias_ref, nc1, topk, db, kc)


def _dec_topk(scores, *, topk, past):
    nc1, db, kc = scores.shape
    return pl.pallas_call(
        functools.partial(_dec_topk_kernel, topk=topk, past=past),
        out_shape=jax.ShapeDtypeStruct((nc1, db, kc), F32),
        scratch_shapes=[pltpu.VMEM((nc1, db, kc), I32)],
        compiler_params=pltpu.CompilerParams(vmem_limit_bytes=VMEM_LIMIT),
        name="dec_topk",
    )(scores)


def _dec_attn_kernel(pt_ref, bias_ref, q_ref, kn_ref, vn_ref, ck_hbm, cv_hbm, o_ref,
                     ring, sem, qb_ref, lg_ref, p_ref, acc_ref, *, n_pages):
    b = pl.program_id(0)
    nb = pl.num_programs(0)
    depth, nh, hd, page = ring.shape

    def page_copy(src_hbm, bb, j, slot):
        return pltpu.make_async_copy(src_hbm.at[0, pt_ref[bb, j]], ring.at[slot], sem.at[slot])

    @pl.when(b == 0)
    def _():
        for j in range(depth):
            page_copy(ck_hbm, 0, j, j).start()

    q = q_ref[0].astype(F32)
    qb_ref[...] = jnp.broadcast_to(q, (nh, hd, page))

    def run(lo, hi, compute, prefetch):
        def body(j, carry):
            slot = lax.rem(j, depth)
            page_copy(ck_hbm, b, j, slot).wait()
            compute(j, slot)
            prefetch(j, slot)
            return carry

        lax.fori_loop(lo, hi, body, 0)

    def key_page(j, slot):
        lg_ref[j] = jnp.sum(ring[slot] * qb_ref[...], axis=1, keepdims=True)

    run(0, n_pages - depth, key_page, lambda j, s: page_copy(ck_hbm, b, j + depth, s).start())
    run(n_pages - depth, n_pages, key_page,
        lambda j, s: page_copy(cv_hbm, b, j + depth - n_pages, s).start())

    bias = bias_ref[:, pl.ds(b, 1), :]
    lg = lg_ref[...] + bias[0:n_pages][:, None]
    l_new = jnp.sum(q * kn_ref[0].astype(F32), axis=1, keepdims=True)
    l_new = jnp.where(bias[n_pages][:, 0:1] == 0.0, l_new, NEG_BIG)
    m = jnp.maximum(jnp.max(jnp.max(lg, axis=0), axis=2, keepdims=True), l_new)
    p = jnp.exp(lg - m[None])
    p_new = jnp.exp(l_new - m)
    p_ref[...] = p
    denom = jnp.sum(jnp.sum(p, axis=0), axis=2, keepdims=True) + p_new
    acc_ref[...] = jnp.zeros(acc_ref.shape, F32)

    def value_page(j, slot):
        acc_ref[...] += p_ref[j] * ring[slot]

    run(0, n_pages - depth, value_page, lambda j, s: page_copy(cv_hbm, b, j + depth, s).start())

    def next_keys(j, s):
        @pl.when(b + 1 < nb)
        def _():
            page_copy(ck_hbm, b + 1, j + depth - n_pages, s).start()

    run(n_pages - depth, n_pages, value_page, next_keys)

    out = jnp.sum(acc_ref[...], axis=2, keepdims=True) + p_new * vn_ref[0].astype(F32)
    o_ref[0] = (out / denom).astype(BF16)


def _dec_attn(page_table, bias, q4, kn4, vn4, cache_k_t, cache_v_t):
    DB, n_pages = page_table.shape
    _, _, nh, hd, page = cache_k_t.shape
    assert n_pages % DEC_RING == 0 and bias.shape == (n_pages + 1, DB, page)
    col = pl.BlockSpec((1, nh, hd, 1), lambda b, pt: (b, 0, 0, 0))
    grid_spec = pltpu.PrefetchScalarGridSpec(
        num_scalar_prefetch=1,
        grid=(DB,),
        in_specs=[pl.BlockSpec(bias.shape, lambda b, pt: (0, 0, 0)), col, col, col,
                  pl.BlockSpec(memory_space=pl.ANY), pl.BlockSpec(memory_space=pl.ANY)],
        out_specs=col,
        scratch_shapes=[pltpu.VMEM((DEC_RING, nh, hd, page), F32), pltpu.SemaphoreType.DMA((DEC_RING,)),
                        pltpu.VMEM((nh, hd, page), F32),
                        pltpu.VMEM((n_pages, nh, 1, page), F32), pltpu.VMEM((n_pages, nh, 1, page), F32),
                        pltpu.VMEM((nh, hd, page), F32)],
    )
    return pl.pallas_call(
        functools.partial(_dec_attn_kernel, n_pages=n_pages),
        grid_spec=grid_spec,
        out_shape=jax.ShapeDtypeStruct((DB, nh, hd, 1), BF16),
        compiler_params=_cparams(("arbitrary",)),
        name="dec_attn",
    )(page_table, bias, q4, kn4, vn4, cache_k_t, cache_v_t)


def _prep_params(w_in, w_out, attn_norm, ffn_norm, q_norm, k_norm, kidx_norm, pool_w, pool_scale,
                 router_group_w, router_group_b, router_expert_w, router_expert_b,
                 expert_w_gate, expert_w_up, expert_w_down):
    d = w_in.shape[0]
    off_ki = 512 * 5
    w_main = w_in[:, :off_ki]
    w_ki = w_in[:, off_ki:off_ki + IDX_DIM]
    w_wi = w_in[:, off_ki + IDX_DIM:]
    w_cat = jnp.concatenate(
        [w_main, w_ki, w_ki, w_wi, jnp.zeros((d, LANES - w_wi.shape[1]), w_in.dtype)], axis=1)
    n_heads = 512 // HEAD_DIM
    gidx = jnp.arange(256) // HEAD_DIM
    gmat = (gidx[:, None] == gidx[None, :]).astype(F32) * (1.0 / HEAD_DIM)
    rw = jnp.concatenate([router_group_w, router_expert_w], axis=1)
    rw = jnp.pad(rw, ((0, 0), (0, LANES - rw.shape[1])))
    rw_hi = rw.astype(BF16)
    rb = jnp.concatenate([router_group_b, router_expert_b])
    rb = jnp.pad(rb, (0, LANES - rb.shape[0]))[None, :]
    return dict(
        w_in=w_cat.astype(BF16),
        attn_g=attn_norm[None, :],
        q_g=jnp.tile(q_norm, n_heads)[None, :],
        k_g=jnp.tile(k_norm, n_heads)[None, :],
        ki_g=jnp.tile(kidx_norm, 2)[None, :],
        gmat=gmat.astype(BF16),
        pool_w=pool_w.astype(BF16),
        pool_s=pool_scale[None, :],
        w_out=w_out.astype(BF16),
        ffn_g=ffn_norm[None, :],
        rw_hi=rw_hi,
        rw_lo=(rw - rw_hi.astype(F32)).astype(BF16),
        rb=rb,
        w_gate=expert_w_gate.astype(BF16),
        w_up=expert_w_up.astype(BF16),
        w_down=expert_w_down.astype(BF16),
    )


def _pad_rows(a, n):
    return jnp.pad(a, ((0, n - a.shape[0]), (0, 0)))


def kernel(x_prompt, x_sample, cache_k, cache_v, cache_kidx, state_pool, page_table, meta_tokens, w_in, w_out, attn_norm, ffn_norm, q_norm, k_norm, kidx_norm, pool_w, pool_scale, router_group_w, router_group_b, router_expert_w, router_expert_b, expert_w_gate, expert_w_up, expert_w_down):
    B, S, D = x_prompt.shape
    DB, DS, _ = x_sample.shape
    depth = w_in.shape[0]
    assert depth == 1 and DS == 1
    n_heads = 512 // HEAD_DIM
    p = _prep_params(w_in[0], w_out[0], attn_norm[0], ffn_norm[0], q_norm[0], k_norm[0], kidx_norm[0],
                     pool_w[0], pool_scale[0], router_group_w[0], router_group_b[0],
                     router_expert_w[0], router_expert_b[0], expert_w_gate[0], expert_w_up[0],
                     expert_w_down[0])
    tq = min(256, S)
    tm_proj = min(512, S)
    topk_p = min(TOPK_MAX, (S + N_META) // 4)
    assert S % tq == 0 and S % tm_proj == 0, "prompt length must be a multiple of the tile sizes"

    zeros_hist = jnp.zeros((16, 512), F32)
    x_meta = jnp.concatenate([jnp.zeros((META_PAD - N_META, D), F32), meta_tokens.astype(F32)], axis=0)
    pm = _project(x_meta[None], p, zeros_hist, tm=META_PAD, with_pool=True)
    pp = _project(x_prompt, p, pm["ulast"][0], tm=tm_proj, with_pool=True)
    pmeta = {k: pm[k][0] for k in ("kb", "vb", "ki2")}
    o = _prompt_attention(pp, pmeta, tq=tq, topk=topk_p)
    n_tok = B * S
    tm_tok = min(512, n_tok)
    x1, xn2, gate = _mix_router(pp["yp"].reshape(n_tok, 512), o.reshape(n_tok, 512),
                                x_prompt.reshape(n_tok, D), p, tm=tm_tok)
    y_prompt = _moe(xn2, gate, x1, p, tm=min(1024, n_tok)).reshape(B, S, D)

    def with_meta(a_meta, a):
        a_meta = a_meta[:, META_PAD - N_META:]
        full = jnp.concatenate([jnp.broadcast_to(a_meta, (B,) + a_meta.shape[1:]), a], axis=1)
        return full

    new_k_prompt = with_meta(pm["kf"], pp["kf"]).reshape(1, B, S + N_META, n_heads, HEAD_DIM)
    new_v_prompt = with_meta(pm["vf"], pp["vf"]).reshape(1, B, S + N_META, n_heads, HEAD_DIM)
    new_kidx_prompt = with_meta(pm["kif"], pp["kif"])[None]
    new_pool_prompt = pp["ulast"][:, 1:, :][None]

    n_pages = page_table.shape[1]
    page = cache_k.shape[2]
    past = n_pages * page
    topk_s = min(TOPK_MAX, (past + DS) // 4)
    ps = _project(x_sample.reshape(1, DB, D), p, zeros_hist, tm=DB, with_pool=False)
    u_s = ps["yp"][0]
    yp_s = _dec_pool(jnp.swapaxes(state_pool[0], 0, 1), u_s, p)
    scores = _dec_scores(page_table, ps["qi"][0].reshape(DB, N_IDX_HEADS, IDX_DIM),
                         ps["wi"][0][:, :N_IDX_HEADS].reshape(DB, N_IDX_HEADS, 1),
                         ps["ki2"][0][:, :IDX_DIM].reshape(DB, 1, IDX_DIM),
                         jnp.swapaxes(cache_kidx, 2, 3), kc=page)
    bias_s = _dec_topk(scores, topk=topk_s, past=past)
    col = lambda a: a[0].reshape(DB, n_heads, HEAD_DIM, 1)
    o_s = _dec_attn(page_table, bias_s, col(ps["q"]), col(ps["kb"]), col(ps["vb"]),
                    jnp.transpose(cache_k, (0, 1, 3, 4, 2)), jnp.transpose(cache_v, (0, 1, 3, 4, 2)))
    x1_s, xn2_s, gate_s = _mix_router(yp_s, o_s.reshape(DB, 512), x_sample.reshape(DB, D), p, tm=DB)
    y_sample = _moe(xn2_s, gate_s, x1_s, p, tm=DB).reshape(DB, DS, D)
    new_k_sample = ps["kf"].reshape(1, DB, DS, n_heads, HEAD_DIM)
    new_v_sample = ps["vf"].reshape(1, DB, DS, n_heads, HEAD_DIM)
    new_kidx_sample = ps["kif"].reshape(1, DB, DS, IDX_DIM)
    new_pool_sample = jnp.concatenate([state_pool[0][:, 1:, :], u_s[:, None, :]], axis=1)[None]
    return (y_prompt, y_sample, new_k_prompt, new_v_prompt, new_kidx_prompt, new_pool_prompt,
            new_k_sample, new_v_sample, new_kidx_sample, new_pool_sample)
```

```python
import functools

import jax
import jax.numpy as jnp
from jax import lax
from jax.experimental import pallas as pl
from jax.experimental.pallas import tpu as pltpu

F32 = jnp.float32
BF16 = jnp.bfloat16
I32 = jnp.int32

N_META = 16
POOL_WINDOWS = (2, 4, 8, 16)
HEAD_DIM = 64
N_IDX_HEADS = 8
IDX_DIM = 64
TOPK_MAX = 256
N_EXPERT_GROUPS = 4
EXPERTS_PER_GROUP = 4
EPS = 1e-6

LANES = 128
LOG2E = 1.4426950408889634
V_ROWS = HEAD_DIM + 16
META_PAD = 128
F32_MAX = 3.4028234663852886e38
DEC_RING = 32
INT_MIN = -(2 ** 31)
NEG_BIG = -1e30
VMEM_LIMIT = 56 * 1024 * 1024


def _cparams(sem):
    return pltpu.CompilerParams(dimension_semantics=sem, vmem_limit_bytes=VMEM_LIMIT)


def _nt_dot(a, b):
    return lax.dot_general(a, b, (((1,), (1,)), ((), ())), preferred_element_type=F32)


def _dot(a, b):
    return jnp.dot(a, b, preferred_element_type=F32)


def _group_mean_sq(hx, g_ref):
    sq = hx * hx
    hi = sq.astype(BF16)
    lo = (sq - hi.astype(F32)).astype(BF16)
    g = g_ref[...]
    halves = []
    for j in range(2):
        sl = slice(j * 256, (j + 1) * 256)
        halves.append(_dot(hi[:, sl], g) + _dot(lo[:, sl], g))
    return jnp.concatenate(halves, axis=1)


def _proj_kernel(x_ref, g_ref, w_ref, qg_ref, kg_ref, kig_ref, gm_ref, pw_ref, ps_ref, uh_ref,
                 yp_ref, q_ref, kb_ref, vb_ref, qi_ref, ki2_ref, wi_ref, kf_ref, vf_ref, kif_ref, ul_ref,
                 uext_ref, *, with_pool):
    i = pl.program_id(1)
    tm = x_ref.shape[1]
    x = x_ref[0]
    ms = jnp.mean(x * x, axis=-1, keepdims=True)
    xn = (x * lax.rsqrt(ms + EPS) * g_ref[...]).astype(BF16)
    h = _dot(xn, w_ref[...])

    u = h[:, 0:512]
    if with_pool:
        @pl.when(i == 0)
        def _():
            uext_ref[0:16, :] = uh_ref[...]

        uext_ref[16:16 + tm, :] = u
        for g, w in enumerate(POOL_WINDOWS):
            cs = slice(g * LANES, (g + 1) * LANES)
            acc = u[:, cs]
            for j in range(1, w):
                acc = acc + uext_ref[16 - j:16 - j + tm, cs]
            m = (acc * (1.0 / w) - u[:, cs]).astype(BF16)
            y = _dot(m, pw_ref[g]) * ps_ref[:, cs]
            yp_ref[0, :, cs] = y.astype(BF16)
        ul_ref[0] = uext_ref[tm:tm + 16, :]
        uext_ref[0:16, :] = uext_ref[tm:tm + 16, :]
    else:
        yp_ref[0] = u
        ul_ref[0] = jnp.zeros(ul_ref.shape[1:], F32)

    hq = h[:, 512:1024]
    qn = hq * lax.rsqrt(_group_mean_sq(hq, gm_ref) + EPS) * qg_ref[...]
    hk = h[:, 1024:1536]
    kn = hk * lax.rsqrt(_group_mean_sq(hk, gm_ref) + EPS) * kg_ref[...]
    kf_ref[0] = kn
    kb_ref[0] = kn.astype(BF16)
    hv = h[:, 1536:2048]
    vf_ref[0] = hv
    qi = h[:, 2048:2560] * (IDX_DIM ** -0.5)
    hki = h[:, 2560:2688]
    kin = hki * lax.rsqrt(jnp.mean(hki * hki, axis=-1, keepdims=True) + EPS) * kig_ref[...]
    kif_ref[0] = kin[:, 0:IDX_DIM]
    wi = h[:, 2688:2816] * (N_IDX_HEADS ** -0.5)

    if with_pool:
        q_ref[0] = (qn * (HEAD_DIM ** -0.5 * LOG2E)).T.astype(BF16)
        qi_ref[0] = qi.T.astype(BF16)
        wi_ref[0] = wi.T[0:N_IDX_HEADS, :]
        ki2_ref[0] = kin[:, 0:IDX_DIM].astype(BF16)
        hvt = hv.T.astype(BF16)
        for hd in range(512 // HEAD_DIM):
            r0 = hd * V_ROWS
            vb_ref[0, r0:r0 + HEAD_DIM, :] = hvt[hd * HEAD_DIM:(hd + 1) * HEAD_DIM, :]
            vb_ref[0, r0 + HEAD_DIM:r0 + V_ROWS, :] = jnp.ones((V_ROWS - HEAD_DIM, tm), BF16)
    else:
        q_ref[0] = (qn * (HEAD_DIM ** -0.5)).astype(BF16)
        qi_ref[0] = qi.astype(BF16)
        wi_ref[0] = wi
        ki2_ref[0] = kin.astype(BF16)
        vb_ref[0] = hv.astype(BF16)


def _project(x, p, u_hist, *, tm, with_pool):
    B, T, D = x.shape
    nt = T // tm
    tile = lambda w: pl.BlockSpec((1, tm, w), lambda b, i: (b, i, 0))
    const2 = lambda a: pl.BlockSpec(a.shape, lambda b, i: (0,) * a.ndim)
    ins = [x, p["attn_g"], p["w_in"], p["q_g"], p["k_g"], p["ki_g"], p["gmat"], p["pool_w"], p["pool_s"], u_hist]
    in_specs = [tile(D)] + [const2(a) for a in ins[1:]]
    sds = jax.ShapeDtypeStruct
    ttile = lambda r: pl.BlockSpec((1, r, tm), lambda b, i: (b, 0, i))
    if with_pool:
        v_rows = (512 // HEAD_DIM) * V_ROWS
        out_shape = [
            sds((B, T, 512), BF16),
            sds((B, 512, T), BF16),
            sds((B, T, 512), BF16),
            sds((B, v_rows, T), BF16),
            sds((B, 512, T), BF16),
            sds((B, T, IDX_DIM), BF16),
            sds((B, N_IDX_HEADS, T), F32),
        ]
        out_specs = [tile(512), ttile(512), tile(512), ttile(v_rows), ttile(512), tile(IDX_DIM),
                     ttile(N_IDX_HEADS)]
    else:
        out_shape = [
            sds((B, T, 512), F32),
            sds((B, T, 512), BF16),
            sds((B, T, 512), BF16),
            sds((B, T, 512), BF16),
            sds((B, T, 512), BF16),
            sds((B, T, 128), BF16),
            sds((B, T, 128), F32),
        ]
        out_specs = [tile(512)] * 5 + [tile(128), tile(128)]
    out_shape += [sds((B, T, 512), F32), sds((B, T, 512), F32), sds((B, T, IDX_DIM), F32),
                  sds((B, 16, 512), F32)]
    out_specs += [tile(512), tile(512), tile(IDX_DIM), pl.BlockSpec((1, 16, 512), lambda b, i: (b, 0, 0))]
    names = ["yp", "q", "kb", "vb", "qi", "ki2", "wi", "kf", "vf", "kif", "ulast"]
    outs = pl.pallas_call(
        functools.partial(_proj_kernel, with_pool=with_pool),
        grid=(B, nt),
        in_specs=in_specs,
        out_specs=out_specs,
        out_shape=out_shape,
        scratch_shapes=[pltpu.VMEM((tm + 16, 512), F32)],
        compiler_params=_cparams(("arbitrary", "arbitrary")),
        name="proj",
    )(*ins)
    return dict(zip(names, outs))


def _sortable_key(s):
    bits = pltpu.bitcast(s, I32)
    bits = jnp.where(bits == INT_MIN, 0, bits)
    return bits ^ ((bits >> 31) & 0x7FFFFFFF)


def _select_topk(key_ref, bias_ref, nc, topk, tq, kc):
    def count(pred):
        def body(c, acc):
            return acc + jnp.where(pred(c, key_ref[c]), 1.0, 0.0)

        acc = lax.fori_loop(0, nc, body, jnp.zeros((tq, kc), F32))
        return jnp.sum(acc, axis=1, keepdims=True)

    kf = float(topk)

    def bit_body(j, t_u):
        cand_u = t_u | jnp.left_shift(jnp.int32(1), 31 - j)
        cand_s = cand_u ^ INT_MIN
        cnt = count(lambda c, k: k >= cand_s)
        return jnp.where(cnt >= kf, cand_u, t_u)

    t_u = lax.fori_loop(0, 32, bit_body, jnp.zeros((tq, 1), I32))
    t = t_u ^ INT_MIN
    cnt_ge = count(lambda c, k: k >= t)
    has_tie = jnp.max(jnp.where(t > INT_MIN, cnt_ge, 0.0)) > kf

    def write_bias(sel_fn):
        def body(c, carry):
            k = key_ref[c]
            bias_ref[c] = jnp.where(sel_fn(c, k) & (k > INT_MIN), 0.0, NEG_BIG)
            return carry

        lax.fori_loop(0, nc, body, 0)

    @pl.when(jnp.logical_not(has_tie))
    def _():
        write_bias(lambda c, k: k >= t)

    @pl.when(has_tie)
    def _():
        lane = lax.broadcasted_iota(I32, (tq, kc), 1)
        need = kf - count(lambda c, k: k > t)

        def pos_body(j, p):
            cand = p | jnp.left_shift(jnp.int32(1), 15 - j)
            cnt = count(lambda c, k: (k == t) & (c * kc + lane < cand))
            return jnp.where(cnt <= need, cand, p)

        p = lax.fori_loop(0, 16, pos_body, jnp.zeros((tq, 1), I32))
        write_bias(lambda c, k: (k > t) | ((k == t) & (c * kc + lane < p)))


def _key_to_f32(ku):
    ks = ku ^ INT_MIN
    return lax.bitcast_convert_type(ks ^ ((ks >> 31) & 0x7FFFFFFF), F32)


def _attn_kernel(qT_ref, qiT_ref, wiT_ref, kb_ref, vT_ref, ki_ref, kbm_ref, vTm_ref, kim_ref, o_ref,
                 sc_ref, scm_ref, hi_ref, him_ref, bias_ref, biasm_ref, qm_ref, lg_ref, p_ref, acc_ref,
                 m_ref, l_ref, a_ref,
                 *, topk, n_meta):
    i = pl.program_id(1)
    tq = qT_ref.shape[2]
    kc = tq
    mp = kbm_ref.shape[0]
    nh = 512 // HEAD_DIM
    nc = i + 1
    kf = float(topk)
    ninf = -jnp.inf
    wiT = wiT_ref[0]
    rowm = lax.broadcasted_iota(I32, (mp, tq), 0)
    rowc = lax.broadcasted_iota(I32, (kc, tq), 0)
    colc = lax.broadcasted_iota(I32, (kc, tq), 1)

    def rows_of(c):
        return pl.ds(pl.multiple_of(c * kc, kc), kc)

    def scores(ki_chunk):
        s = None
        for h in range(N_IDX_HEADS):
            d = _dot(ki_chunk, qiT_ref[0, h * IDX_DIM:(h + 1) * IDX_DIM, :])
            t = jnp.maximum(d, 0.0) * wiT[h:h + 1, :]
            s = t if s is None else s + t
        return s

    def coarse(s):
        bits = jnp.where(s == 0.0, 0, lax.bitcast_convert_type(s, I32))
        return lax.bitcast_convert_type(bits & -65536, F32).astype(BF16)

    def put_scores(s, s_dst, h_dst):
        s_dst[...] = s
        h_dst[...] = coarse(s)

    put_scores(jnp.where(rowm >= mp - n_meta, scores(kim_ref[...]), ninf), scm_ref, him_ref)

    def score_body(c, carry):
        put_scores(scores(ki_ref[0, rows_of(c), :]), sc_ref.at[c], hi_ref.at[c])
        return carry

    lax.fori_loop(0, i, score_body, 0)
    put_scores(jnp.where(rowc <= colc, scores(ki_ref[0, rows_of(i), :]), ninf), sc_ref.at[i], hi_ref.at[i])
    put_scores(jnp.full((kc, tq), ninf, F32), sc_ref.at[nc], hi_ref.at[nc])

    half = (nc + 1) // 2

    def count_coarse(cand):
        one = jnp.ones((), BF16)
        zero = jnp.zeros((), BF16)

        def hits(hv):
            m = jnp.where(hv >= cand, one, zero)
            parts = [m[r * 16:(r + 1) * 16] for r in range(hv.shape[0] // 16)]
            while len(parts) > 1:
                parts = [parts[a] + parts[a + 1] for a in range(0, len(parts), 2)]
            return parts[0]

        def body(c, acc):
            return acc + (hits(hi_ref[c]) + hits(hi_ref[c + half])).astype(F32)

        acc = lax.fori_loop(0, half, body, hits(him_ref[...]).astype(F32))
        return jnp.sum(acc, axis=0, keepdims=True)

    def count(pred):
        def hits(s, p0):
            return jnp.sum(jnp.where(pred(s, p0), 1.0, 0.0).reshape(s.shape[0] // 8, 8, tq), axis=0)

        def body(c, acc):
            c2 = c + half
            return acc + hits(sc_ref[c], mp + c * kc) + hits(sc_ref[c2], mp + c2 * kc)

        return jnp.sum(lax.fori_loop(0, half, body, hits(scm_ref[...], 0)), axis=0, keepdims=True)

    def coarse_bit(j, t_u):
        cand_u = t_u | jnp.left_shift(jnp.int32(1), 31 - j)
        cand_bits = lax.bitcast_convert_type(_key_to_f32(cand_u), I32) & -65536
        cand = lax.bitcast_convert_type(cand_bits, F32).astype(BF16)
        return jnp.where(count_coarse(cand) >= kf, cand_u, t_u)

    def fine_bit(j, t_u):
        cand_u = t_u | jnp.left_shift(jnp.int32(1), 31 - j)
        cand = _key_to_f32(cand_u)
        cnt = count(lambda s, p0: s >= cand)
        return jnp.where(cnt >= kf, cand_u, t_u)

    t_u = lax.fori_loop(0, 16, coarse_bit, jnp.zeros((1, tq), I32))
    t_u = lax.fori_loop(16, 32, fine_bit, t_u)
    lane1 = lax.broadcasted_iota(I32, (1, tq), 1)
    n_valid = (i * tq + lane1 + (1 + n_meta)).astype(F32)
    t = jnp.where(n_valid > kf, _key_to_f32(t_u), -F32_MAX)
    cnt_ge = count(lambda s, p0: s >= t)
    has_tie = jnp.max(jnp.where(n_valid > kf, cnt_ge, 0.0)) > kf

    def write_bias(sel):
        biasm_ref[...] = jnp.where(sel(scm_ref[...], 0), 0.0, NEG_BIG)

        def body(c, carry):
            bias_ref[c] = jnp.where(sel(sc_ref[c], mp + c * kc), 0.0, NEG_BIG)
            return carry

        lax.fori_loop(0, nc, body, 0)

    @pl.when(jnp.logical_not(has_tie))
    def _():
        write_bias(lambda s, p0: s >= t)

    @pl.when(has_tie)
    def _():
        need = kf - count(lambda s, p0: s > t)

        def pos_of(s, p0):
            return p0 + lax.broadcasted_iota(I32, s.shape, 0)

        def pos_body(j, p):
            cand = p | jnp.left_shift(jnp.int32(1), 15 - j)
            cnt = count(lambda s, p0: (s == t) & (pos_of(s, p0) < cand))
            return jnp.where(cnt <= need, cand, p)

        p = lax.fori_loop(0, 16, pos_body, jnp.zeros((1, tq), I32))
        write_bias(lambda s, p0: (s > t) | ((s == t) & (pos_of(s, p0) < p)))

    sub = lax.broadcasted_iota(I32, (LANES, tq), 0)
    for h in range(nh):
        blk = qT_ref[0, (h // 2) * LANES:(h // 2 + 1) * LANES, :]
        keep = (sub < HEAD_DIM) if h % 2 == 0 else (sub >= HEAD_DIM)
        qm_ref[h] = jnp.where(keep, blk, jnp.zeros_like(blk))
    m_ref[...] = jnp.full(m_ref.shape, NEG_BIG, F32)
    l_ref[...] = jnp.zeros(l_ref.shape, F32)
    acc_ref[...] = jnp.zeros(acc_ref.shape, F32)

    def attend(k_of, v_of, bias, n):
        for h in range(nh):
            pr = slice((h // 2) * LANES, (h // 2 + 1) * LANES)
            lg = _dot(k_of(pr), qm_ref[h]) + bias
            lg_ref[h, 0:n, :] = lg
            m_old = m_ref[h:h + 1, :]
            m_new = jnp.maximum(m_old, jnp.max(lg, axis=0, keepdims=True))
            a_ref[h:h + 1, :] = jnp.exp2(m_old - m_new)
            m_ref[h:h + 1, :] = m_new
        for h in range(nh):
            p_ref[h, 0:n, :] = jnp.exp2(lg_ref[h, 0:n, :] - m_ref[h:h + 1, :]).astype(BF16)
        for h in range(nh):
            pv = _dot(v_of(slice(h * V_ROWS, (h + 1) * V_ROWS)), p_ref[h, 0:n, :])
            hs = slice(h * HEAD_DIM, (h + 1) * HEAD_DIM)
            alpha = a_ref[h:h + 1, :]
            acc_ref[hs, :] = alpha * acc_ref[hs, :] + pv[0:HEAD_DIM, :]
            l_ref[h:h + 1, :] = alpha * l_ref[h:h + 1, :] + pv[HEAD_DIM:HEAD_DIM + 1, :]

    attend(lambda pr: kbm_ref[:, pr], lambda vs: vTm_ref[vs, :], biasm_ref[...], mp)

    def attn_body(c, carry):
        rows = rows_of(c)
        attend(lambda pr: kb_ref[0, rows, pr], lambda vs: vT_ref[0, vs, rows], bias_ref[c], kc)
        return carry

    lax.fori_loop(0, nc, attn_body, 0)

    for h in range(nh):
        hs = slice(h * HEAD_DIM, (h + 1) * HEAD_DIM)
        acc_ref[hs, :] = acc_ref[hs, :] / l_ref[h:h + 1, :]
    o_ref[0] = acc_ref[...].T.astype(BF16)


def _prompt_attention(pp, pm, *, tq, topk):
    B, T, _ = pp["kb"].shape
    nq = T // tq
    nh = 512 // HEAD_DIM
    mp = pm["kb"].shape[0]
    ttile = lambda r: pl.BlockSpec((1, r, tq), lambda b, i: (b, 0, i))
    whole = lambda a: pl.BlockSpec((1,) + a.shape[1:], lambda b, i: (b, 0, 0))
    const = lambda a: pl.BlockSpec(a.shape, lambda b, i: (0, 0))
    return pl.pallas_call(
        functools.partial(_attn_kernel, topk=topk, n_meta=N_META),
        grid=(B, nq),
        in_specs=[ttile(512), ttile(512), ttile(N_IDX_HEADS), whole(pp["kb"]), whole(pp["vb"]),
                  whole(pp["ki2"]), const(pm["kb"]), const(pm["vb"]), const(pm["ki2"])],
        out_specs=pl.BlockSpec((1, tq, 512), lambda b, i: (b, i, 0)),
        out_shape=jax.ShapeDtypeStruct((B, T, 512), BF16),
        scratch_shapes=[pltpu.VMEM((nq + 1, tq, tq), F32), pltpu.VMEM((mp, tq), F32),
                        pltpu.VMEM((nq + 1, tq, tq), BF16), pltpu.VMEM((mp, tq), BF16),
                        pltpu.VMEM((nq, tq, tq), F32), pltpu.VMEM((mp, tq), F32),
                        pltpu.VMEM((nh, LANES, tq), BF16),
                        pltpu.VMEM((nh, tq, tq), F32), pltpu.VMEM((nh, tq, tq), BF16),
                        pltpu.VMEM((512, tq), F32),
                        pltpu.VMEM((nh, tq), F32), pltpu.VMEM((nh, tq), F32), pltpu.VMEM((nh, tq), F32)],
        compiler_params=_cparams(("arbitrary", "arbitrary")),
        name="attn",
    )(pp["q"], pp["qi"], pp["wi"], pp["kb"], pp["vb"], pp["ki2"], pm["kb"], pm["vb"], pm["ki2"])


def _split_dot3(x, whi_ref, wlo_ref):
    xh = x.astype(BF16)
    xl = (x - xh.astype(F32)).astype(BF16)
    return _dot(xh, whi_ref[...]) + _dot(xl, whi_ref[...]) + _dot(xh, wlo_ref[...])


def _mix_router_kernel(yp_ref, o_ref, x_ref, wo_ref, g_ref, rwh_ref, rwl_ref, rb_ref,
                       x1_ref, xn_ref, gate_ref):
    mix = _dot(yp_ref[...], wo_ref[0:512, :]) + _dot(o_ref[...], wo_ref[512:1024, :])
    x1 = x_ref[...] + mix
    x1_ref[...] = x1
    ms = jnp.mean(x1 * x1, axis=-1, keepdims=True)
    xn = x1 * lax.rsqrt(ms + EPS) * g_ref[...]
    xn_ref[...] = xn.astype(BF16)

    lg = _split_dot3(xn, rwh_ref, rwl_ref) + rb_ref[...]
    tm = lg.shape[0]
    lane = lax.broadcasted_iota(I32, (tm, LANES), 1)
    ninf = -jnp.inf
    gl = jnp.where(lane < N_EXPERT_GROUPS, lg, ninf)
    gmax = jnp.max(gl, axis=1, keepdims=True)
    p_sel = 1.0 / jnp.sum(jnp.exp(gl - gmax), axis=1, keepdims=True)
    g_sel = jnp.min(jnp.where(gl == gmax, lane, LANES), axis=1, keepdims=True)
    e0 = N_EXPERT_GROUPS + g_sel * EXPERTS_PER_GROUP
    ev = jnp.where((lane >= e0) & (lane < e0 + EXPERTS_PER_GROUP), lg, ninf)
    m1 = jnp.max(ev, axis=1, keepdims=True)
    i1 = jnp.min(jnp.where(ev == m1, lane, LANES), axis=1, keepdims=True)
    ev2 = jnp.where(lane == i1, ninf, ev)
    m2 = jnp.max(ev2, axis=1, keepdims=True)
    i2 = jnp.min(jnp.where(ev2 == m2, lane, LANES), axis=1, keepdims=True)
    e2 = jnp.exp(m2 - m1)
    inv = p_sel / (1.0 + e2)
    gate_ref[...] = jnp.where(lane == i1, inv, 0.0) + jnp.where(lane == i2, e2 * inv, 0.0)


def _mix_router(yp, o, x, p, *, tm):
    N, D = x.shape
    row = lambda w: pl.BlockSpec((tm, w), lambda i: (i, 0))
    const = lambda a: pl.BlockSpec(a.shape, lambda i: (0,) * a.ndim)
    consts = [p["w_out"], p["ffn_g"], p["rw_hi"], p["rw_lo"], p["rb"]]
    return pl.pallas_call(
        _mix_router_kernel,
        grid=(N // tm,),
        in_specs=[row(512), row(512), row(D)] + [const(a) for a in consts],
        out_specs=[row(D), row(D), row(LANES)],
        out_shape=[jax.ShapeDtypeStruct((N, D), F32), jax.ShapeDtypeStruct((N, D), BF16),
                   jax.ShapeDtypeStruct((N, LANES), F32)],
        compiler_params=_cparams(("arbitrary",)),
        name="mix_router",
    )(yp, o, x, *consts)


def _moe_kernel(xn_ref, gate_ref, x1_ref, wg_ref, wu_ref, wd_ref, y_ref, acc_ref):
    e = pl.program_id(1)

    @pl.when(e == 0)
    def _():
        acc_ref[...] = jnp.zeros_like(acc_ref)

    xn = xn_ref[...]
    hg = _dot(xn, wg_ref[0])
    hu = _dot(xn, wu_ref[0])
    hact = (hg * (1.0 / (1.0 + jnp.exp(-hg))) * hu).astype(BF16)
    d = _dot(hact, wd_ref[0])
    lane = lax.broadcasted_iota(I32, gate_ref.shape, 1)
    gcol = jnp.sum(jnp.where(lane == e + N_EXPERT_GROUPS, gate_ref[...], 0.0), axis=1, keepdims=True)
    acc_ref[...] += gcol * d

    @pl.when(e == pl.num_programs(1) - 1)
    def _():
        y_ref[...] = x1_ref[...] + acc_ref[...]


def _moe(xn, gate, x1, p, *, tm):
    N, D = x1.shape
    E, _, FF = p["w_gate"].shape
    row = lambda w: pl.BlockSpec((tm, w), lambda i, e: (i, 0))
    return pl.pallas_call(
        _moe_kernel,
        grid=(N // tm, E),
        in_specs=[row(D), row(LANES), row(D),
                  pl.BlockSpec((1, D, FF), lambda i, e: (e, 0, 0)),
                  pl.BlockSpec((1, D, FF), lambda i, e: (e, 0, 0)),
                  pl.BlockSpec((1, FF, D), lambda i, e: (e, 0, 0))],
        out_specs=row(D),
        out_shape=jax.ShapeDtypeStruct((N, D), F32),
        scratch_shapes=[pltpu.VMEM((tm, D), F32)],
        compiler_params=_cparams(("arbitrary", "arbitrary")),
        name="moe",
    )(xn, gate, x1, p["w_gate"], p["w_up"], p["w_down"])


def _dec_pool_kernel(st_ref, u_ref, pw_ref, ps_ref, yp_ref):
    n_prev = st_ref.shape[0]
    u = u_ref[...]
    for g, w in enumerate(POOL_WINDOWS):
        cs = slice(g * LANES, (g + 1) * LANES)
        acc = u[:, cs]
        for j in range(1, w):
            acc = acc + st_ref[n_prev - j, :, cs]
        m = (acc * (1.0 / w) - u[:, cs]).astype(BF16)
        yp_ref[:, cs] = (_dot(m, pw_ref[g]) * ps_ref[:, cs]).astype(BF16)


def _dec_pool(state_t, u, p):
    return pl.pallas_call(
        _dec_pool_kernel,
        out_shape=jax.ShapeDtypeStruct(u.shape, BF16),
        compiler_params=pltpu.CompilerParams(vmem_limit_bytes=VMEM_LIMIT),
        name="dec_pool",
    )(state_t, u, p["pool_w"], p["pool_s"])


def _dec_scores_kernel(pt_ref, qi_ref, w_ref, kin_ref, kidx_hbm, out_ref, buf, sem, *, n_pages, page, kc):
    b = pl.program_id(0)
    nb = pl.num_programs(0)

    def copies(bb, slot):
        return [pltpu.make_async_copy(kidx_hbm.at[0, pt_ref[bb, pg]],
                                      buf.at[slot, :, pl.ds(pg * page, page)], sem.at[slot])
                for pg in range(n_pages)]

    @pl.when(b == 0)
    def _():
        for c in copies(0, 0):
            c.start()

    @pl.when(b + 1 < nb)
    def _():
        for c in copies(b + 1, (b + 1) % 2):
            c.start()

    slot = b % 2
    for c in copies(b, slot):
        c.wait()

    qi = qi_ref[0]
    w = w_ref[0]
    past = n_pages * page
    dots = _dot(qi, buf[slot].astype(BF16))
    s = jnp.sum(jnp.maximum(dots, 0.0) * w, axis=0, keepdims=True)
    for c in range(past // kc):
        out_ref[c, pl.ds(b, 1), :] = s[:, c * kc:(c + 1) * kc]
    d_new = jnp.sum(qi.astype(F32) * kin_ref[0].astype(F32), axis=1, keepdims=True)
    s_new = jnp.sum(jnp.maximum(d_new, 0.0) * w, axis=0, keepdims=True)
    lane = lax.broadcasted_iota(I32, (1, kc), 1)
    out_ref[past // kc, pl.ds(b, 1), :] = jnp.where(lane == 0, s_new, -jnp.inf)


def _dec_scores(page_table, qi3, w3, kin3, cache_kidx_t, *, kc):
    DB, n_pages = page_table.shape
    page = cache_kidx_t.shape[3]
    past = n_pages * page
    nc1 = past // kc + 1
    grid_spec = pltpu.PrefetchScalarGridSpec(
        num_scalar_prefetch=1,
        grid=(DB,),
        in_specs=[pl.BlockSpec((1, N_IDX_HEADS, IDX_DIM), lambda b, pt: (b, 0, 0)),
                  pl.BlockSpec((1, N_IDX_HEADS, 1), lambda b, pt: (b, 0, 0)),
                  pl.BlockSpec((1, 1, IDX_DIM), lambda b, pt: (b, 0, 0)),
                  pl.BlockSpec(memory_space=pl.ANY)],
        out_specs=pl.BlockSpec((nc1, DB, kc), lambda b, pt: (0, 0, 0)),
        scratch_shapes=[pltpu.VMEM((2, IDX_DIM, past), F32), pltpu.SemaphoreType.DMA((2,))],
    )
    return pl.pallas_call(
        functools.partial(_dec_scores_kernel, n_pages=n_pages, page=page, kc=kc),
        grid_spec=grid_spec,
        out_shape=jax.ShapeDtypeStruct((nc1, DB, kc), F32),
        compiler_params=_cparams(("arbitrary",)),
        name="dec_scores",
    )(page_table, qi3, w3, kin3, cache_kidx_t)


def _dec_topk_kernel(sc_ref, bias_ref, key_ref, *, topk, past):
    nc1, db, kc = sc_ref.shape
    lane = lax.broadcasted_iota(I32, (db, kc), 1)

    def key_body(c, carry):
        key_ref[c] = jnp.where(c * kc + lane <= past, _sortable_key(sc_ref[c]), INT_MIN)
        return carry

    lax.fori_loop(0, nc1, key_body, 0)
    _select_topk(key_ref, bias_ref, nc1, topk, db, kc)


def _dec_topk(scores, *, topk, past):
    nc1, db, kc = scores.shape
    return pl.pallas_call(
        functools.partial(_dec_topk_kernel, topk=topk, past=past),
        out_shape=jax.ShapeDtypeStruct((nc1, db, kc), F32),
        scratch_shapes=[pltpu.VMEM((nc1, db, kc), I32)],
        compiler_params=pltpu.CompilerParams(vmem_limit_bytes=VMEM_LIMIT),
        name="dec_topk",
    )(scores)


def _dec_attn_kernel(pt_ref, bias_ref, q_ref, kn_ref, vn_ref, ck_hbm, cv_hbm, o_ref,
                     ring, sem, qb_ref, lg_ref, p_ref, acc_ref, *, n_pages):
    b = pl.program_id(0)
    nb = pl.num_programs(0)
    depth, nh, hd, page = ring.shape

    def page_copy(src_hbm, bb, j, slot):
        return pltpu.make_async_copy(src_hbm.at[0, pt_ref[bb, j]], ring.at[slot], sem.at[slot])

    @pl.when(b == 0)
    def _():
        for j in range(depth):
            page_copy(ck_hbm, 0, j, j).start()

    q = q_ref[0].astype(F32)
    qb_ref[...] = jnp.broadcast_to(q, (nh, hd, page))

    def run(lo, hi, compute, prefetch):
        def body(j, carry):
            slot = lax.rem(j, depth)
            page_copy(ck_hbm, b, j, slot).wait()
            compute(j, slot)
            prefetch(j, slot)
            return carry

        lax.fori_loop(lo, hi, body, 0)

    def key_page(j, slot):
        lg_ref[j] = jnp.sum(ring[slot] * qb_ref[...], axis=1, keepdims=True)

    run(0, n_pages - depth, key_page, lambda j, s: page_copy(ck_hbm, b, j + depth, s).start())
    run(n_pages - depth, n_pages, key_page,
        lambda j, s: page_copy(cv_hbm, b, j + depth - n_pages, s).start())

    bias = bias_ref[:, pl.ds(b, 1), :]
    lg = lg_ref[...] + bias[0:n_pages][:, None]
    l_new = jnp.sum(q * kn_ref[0].astype(F32), axis=1, keepdims=True)
    l_new = jnp.where(bias[n_pages][:, 0:1] == 0.0, l_new, NEG_BIG)
    m = jnp.maximum(jnp.max(jnp.max(lg, axis=0), axis=2, keepdims=True), l_new)
    p = jnp.exp(lg - m[None])
    p_new = jnp.exp(l_new - m)
    p_ref[...] = p
    denom = jnp.sum(jnp.sum(p, axis=0), axis=2, keepdims=True) + p_new
    acc_ref[...] = jnp.zeros(acc_ref.shape, F32)

    def value_page(j, slot):
        acc_ref[...] += p_ref[j] * ring[slot]

    run(0, n_pages - depth, value_page, lambda j, s: page_copy(cv_hbm, b, j + depth, s).start())

    def next_keys(j, s):
        @pl.when(b + 1 < nb)
        def _():
            page_copy(ck_hbm, b + 1, j + depth - n_pages, s).start()

    run(n_pages - depth, n_pages, value_page, next_keys)

    out = jnp.sum(acc_ref[...], axis=2, keepdims=True) + p_new * vn_ref[0].astype(F32)
    o_ref[0] = (out / denom).astype(BF16)


def _dec_attn(page_table, bias, q4, kn4, vn4, cache_k_t, cache_v_t):
    DB, n_pages = page_table.shape
    _, _, nh, hd, page = cache_k_t.shape
    depth = min(DEC_RING, n_pages)
    assert n_pages % depth == 0 and bias.shape == (n_pages + 1, DB, page)
    col = pl.BlockSpec((1, nh, hd, 1), lambda b, pt: (b, 0, 0, 0))
    grid_spec = pltpu.PrefetchScalarGridSpec(
        num_scalar_prefetch=1,
        grid=(DB,),
        in_specs=[pl.BlockSpec(bias.shape, lambda b, pt: (0, 0, 0)), col, col, col,
                  pl.BlockSpec(memory_space=pl.ANY), pl.BlockSpec(memory_space=pl.ANY)],
        out_specs=col,
        scratch_shapes=[pltpu.VMEM((depth, nh, hd, page), F32), pltpu.SemaphoreType.DMA((depth,)),
                        pltpu.VMEM((nh, hd, page), F32),
                        pltpu.VMEM((n_pages, nh, 1, page), F32), pltpu.VMEM((n_pages, nh, 1, page), F32),
                        pltpu.VMEM((nh, hd, page), F32)],
    )
    return pl.pallas_call(
        functools.partial(_dec_attn_kernel, n_pages=n_pages),
        grid_spec=grid_spec,
        out_shape=jax.ShapeDtypeStruct((DB, nh, hd, 1), BF16),
        compiler_params=_cparams(("arbitrary",)),
        name="dec_attn",
    )(page_table, bias, q4, kn4, vn4, cache_k_t, cache_v_t)


def _prep_params(w_in, w_out, attn_norm, ffn_norm, q_norm, k_norm, kidx_norm, pool_w, pool_scale,
                 router_group_w, router_group_b, router_expert_w, router_expert_b,
                 expert_w_gate, expert_w_up, expert_w_down):
    d = w_in.shape[0]
    off_ki = 512 * 5
    w_main = w_in[:, :off_ki]
    w_ki = w_in[:, off_ki:off_ki + IDX_DIM]
    w_wi = w_in[:, off_ki + IDX_DIM:]
    w_cat = jnp.concatenate(
        [w_main, w_ki, w_ki, w_wi, jnp.zeros((d, LANES - w_wi.shape[1]), w_in.dtype)], axis=1)
    n_heads = 512 // HEAD_DIM
    gidx = jnp.arange(256) // HEAD_DIM
    gmat = (gidx[:, None] == gidx[None, :]).astype(F32) * (1.0 / HEAD_DIM)
    rw = jnp.concatenate([router_group_w, router_expert_w], axis=1)
    rw = jnp.pad(rw, ((0, 0), (0, LANES - rw.shape[1])))
    rw_hi = rw.astype(BF16)
    rb = jnp.concatenate([router_group_b, router_expert_b])
    rb = jnp.pad(rb, (0, LANES - rb.shape[0]))[None, :]
    return dict(
        w_in=w_cat.astype(BF16),
        attn_g=attn_norm[None, :],
        q_g=jnp.tile(q_norm, n_heads)[None, :],
        k_g=jnp.tile(k_norm, n_heads)[None, :],
        ki_g=jnp.tile(kidx_norm, 2)[None, :],
        gmat=gmat.astype(BF16),
        pool_w=pool_w.astype(BF16),
        pool_s=pool_scale[None, :],
        w_out=w_out.astype(BF16),
        ffn_g=ffn_norm[None, :],
        rw_hi=rw_hi,
        rw_lo=(rw - rw_hi.astype(F32)).astype(BF16),
        rb=rb,
        w_gate=expert_w_gate.astype(BF16),
        w_up=expert_w_up.astype(BF16),
        w_down=expert_w_down.astype(BF16),
    )


def _pad_rows(a, n):
    return jnp.pad(a, ((0, n - a.shape[0]), (0, 0)))


def kernel(x_prompt, x_sample, cache_k, cache_v, cache_kidx, state_pool, page_table, meta_tokens, w_in, w_out, attn_norm, ffn_norm, q_norm, k_norm, kidx_norm, pool_w, pool_scale, router_group_w, router_group_b, router_expert_w, router_expert_b, expert_w_gate, expert_w_up, expert_w_down):
    B, S, D = x_prompt.shape
    DB, DS, _ = x_sample.shape
    depth = w_in.shape[0]
    assert depth == 1 and DS == 1
    n_heads = 512 // HEAD_DIM
    p = _prep_params(w_in[0], w_out[0], attn_norm[0], ffn_norm[0], q_norm[0], k_norm[0], kidx_norm[0],
                     pool_w[0], pool_scale[0], router_group_w[0], router_group_b[0],
                     router_expert_w[0], router_expert_b[0], expert_w_gate[0], expert_w_up[0],
                     expert_w_down[0])
    tq = min(256, S)
    tm_proj = min(512, S)
    topk_p = min(TOPK_MAX, (S + N_META) // 4)
    assert S % tq == 0 and S % tm_proj == 0, "prompt length must be a multiple of the tile sizes"

    zeros_hist = jnp.zeros((16, 512), F32)
    x_meta = jnp.concatenate([jnp.zeros((META_PAD - N_META, D), F32), meta_tokens.astype(F32)], axis=0)
    pm = _project(x_meta[None], p, zeros_hist, tm=META_PAD, with_pool=True)
    pp = _project(x_prompt, p, pm["ulast"][0], tm=tm_proj, with_pool=True)
    pmeta = {k: pm[k][0] for k in ("kb", "vb", "ki2")}
    o = _prompt_attention(pp, pmeta, tq=tq, topk=topk_p)
    n_tok = B * S
    tm_tok = min(512, n_tok)
    x1, xn2, gate = _mix_router(pp["yp"].reshape(n_tok, 512), o.reshape(n_tok, 512),
                                x_prompt.reshape(n_tok, D), p, tm=tm_tok)
    y_prompt = _moe(xn2, gate, x1, p, tm=min(1024, n_tok)).reshape(B, S, D)

    def with_meta(a_meta, a):
        a_meta = a_meta[:, META_PAD - N_META:]
        full = jnp.concatenate([jnp.broadcast_to(a_meta, (B,) + a_meta.shape[1:]), a], axis=1)
        return full

    new_k_prompt = with_meta(pm["kf"], pp["kf"]).reshape(1, B, S + N_META, n_heads, HEAD_DIM)
    new_v_prompt = with_meta(pm["vf"], pp["vf"]).reshape(1, B, S + N_META, n_heads, HEAD_DIM)
    new_kidx_prompt = with_meta(pm["kif"], pp["kif"])[None]
    new_pool_prompt = pp["ulast"][:, 1:, :][None]

    n_pages = page_table.shape[1]
    page = cache_k.shape[2]
    past = n_pages * page
    topk_s = min(TOPK_MAX, (past + DS) // 4)
    ps = _project(x_sample.reshape(1, DB, D), p, zeros_hist, tm=DB, with_pool=False)
    u_s = ps["yp"][0]
    yp_s = _dec_pool(jnp.swapaxes(state_pool[0], 0, 1), u_s, p)
    scores = _dec_scores(page_table, ps["qi"][0].reshape(DB, N_IDX_HEADS, IDX_DIM),
                         ps["wi"][0][:, :N_IDX_HEADS].reshape(DB, N_IDX_HEADS, 1),
                         ps["ki2"][0][:, :IDX_DIM].reshape(DB, 1, IDX_DIM),
                         jnp.swapaxes(cache_kidx, 2, 3), kc=page)
    bias_s = _dec_topk(scores, topk=topk_s, past=past)
    col = lambda a: a[0].reshape(DB, n_heads, HEAD_DIM, 1)
    o_s = _dec_attn(page_table, bias_s, col(ps["q"]), col(ps["kb"]), col(ps["vb"]),
                    jnp.transpose(cache_k, (0, 1, 3, 4, 2)), jnp.transpose(cache_v, (0, 1, 3, 4, 2)))
    x1_s, xn2_s, gate_s = _mix_router(yp_s, o_s.reshape(DB, 512), x_sample.reshape(DB, D), p, tm=DB)
    y_sample = _moe(xn2_s, gate_s, x1_s, p, tm=DB).reshape(DB, DS, D)
    new_k_sample = ps["kf"].reshape(1, DB, DS, n_heads, HEAD_DIM)
    new_v_sample = ps["vf"].reshape(1, DB, DS, n_heads, HEAD_DIM)
    new_kidx_sample = ps["kif"].reshape(1, DB, DS, IDX_DIM)
    new_pool_sample = jnp.concatenate([state_pool[0][:, 1:, :], u_s[:, None, :]], axis=1)[None]
    return (y_prompt, y_sample, new_k_prompt, new_v_prompt, new_kidx_prompt, new_pool_prompt,
            new_k_sample, new_v_sample, new_kidx_sample, new_pool_sample)
```

```python
import functools

import jax
import jax.numpy as jnp
from jax import lax
from jax.experimental import pallas as pl
from jax.experimental.pallas import tpu as pltpu

F32 = jnp.float32
BF16 = jnp.bfloat16
I32 = jnp.int32

N_META = 16
POOL_WINDOWS = (2, 4, 8, 16)
HEAD_DIM = 64
N_IDX_HEADS = 8
IDX_DIM = 64
TOPK_MAX = 256
N_EXPERT_GROUPS = 4
EXPERTS_PER_GROUP = 4
EPS = 1e-6

LANES = 128
LOG2E = 1.4426950408889634
V_ROWS = HEAD_DIM + 16
META_PAD = 128
F32_MAX = 3.4028234663852886e38
DEC_RING = 32
INT_MIN = -(2 ** 31)
NEG_BIG = -1e30
VMEM_LIMIT = 56 * 1024 * 1024


def _cparams(sem):
    return pltpu.CompilerParams(dimension_semantics=sem, vmem_limit_bytes=VMEM_LIMIT)


def _nt_dot(a, b):
    return lax.dot_general(a, b, (((1,), (1,)), ((), ())), preferred_element_type=F32)


def _dot(a, b):
    return jnp.dot(a, b, preferred_element_type=F32)


def _group_mean_sq(hx, g_ref):
    sq = hx * hx
    hi = sq.astype(BF16)
    lo = (sq - hi.astype(F32)).astype(BF16)
    g = g_ref[...]
    halves = []
    for j in range(2):
        sl = slice(j * 256, (j + 1) * 256)
        halves.append(_dot(hi[:, sl], g) + _dot(lo[:, sl], g))
    return jnp.concatenate(halves, axis=1)


def _proj_kernel(x_ref, g_ref, w_ref, qg_ref, kg_ref, kig_ref, gm_ref, pw_ref, ps_ref, uh_ref,
                 yp_ref, q_ref, kb_ref, vb_ref, qi_ref, ki2_ref, wi_ref, kf_ref, vf_ref, kif_ref, ul_ref,
                 uext_ref, *, with_pool):
    i = pl.program_id(1)
    tm = x_ref.shape[1]
    x = x_ref[0]
    ms = jnp.mean(x * x, axis=-1, keepdims=True)
    xn = (x * lax.rsqrt(ms + EPS) * g_ref[...]).astype(BF16)
    h = _dot(xn, w_ref[...])

    u = h[:, 0:512]
    if with_pool:
        @pl.when(i == 0)
        def _():
            uext_ref[0:16, :] = uh_ref[...]

        uext_ref[16:16 + tm, :] = u
        for g, w in enumerate(POOL_WINDOWS):
            cs = slice(g * LANES, (g + 1) * LANES)
            acc = u[:, cs]
            for j in range(1, w):
                acc = acc + uext_ref[16 - j:16 - j + tm, cs]
            m = (acc * (1.0 / w) - u[:, cs]).astype(BF16)
            y = _dot(m, pw_ref[g]) * ps_ref[:, cs]
            yp_ref[0, :, cs] = y.astype(BF16)
        ul_ref[0] = uext_ref[tm:tm + 16, :]
        uext_ref[0:16, :] = uext_ref[tm:tm + 16, :]
    else:
        yp_ref[0] = u
        ul_ref[0] = jnp.zeros(ul_ref.shape[1:], F32)

    hq = h[:, 512:1024]
    qn = hq * lax.rsqrt(_group_mean_sq(hq, gm_ref) + EPS) * qg_ref[...]
    hk = h[:, 1024:1536]
    kn = hk * lax.rsqrt(_group_mean_sq(hk, gm_ref) + EPS) * kg_ref[...]
    kf_ref[0] = kn
    kb_ref[0] = kn.astype(BF16)
    hv = h[:, 1536:2048]
    vf_ref[0] = hv
    qi = h[:, 2048:2560] * (IDX_DIM ** -0.5)
    hki = h[:, 2560:2688]
    kin = hki * lax.rsqrt(jnp.mean(hki * hki, axis=-1, keepdims=True) + EPS) * kig_ref[...]
    kif_ref[0] = kin[:, 0:IDX_DIM]
    wi = h[:, 2688:2816] * (N_IDX_HEADS ** -0.5)

    if with_pool:
        q_ref[0] = (qn * (HEAD_DIM ** -0.5 * LOG2E)).T.astype(BF16)
        qi_ref[0] = qi.T.astype(BF16)
        wi_ref[0] = wi.T[0:N_IDX_HEADS, :]
        ki2_ref[0] = kin[:, 0:IDX_DIM].astype(BF16)
        hvt = hv.T.astype(BF16)
        for hd in range(512 // HEAD_DIM):
            r0 = hd * V_ROWS
            vb_ref[0, r0:r0 + HEAD_DIM, :] = hvt[hd * HEAD_DIM:(hd + 1) * HEAD_DIM, :]
            vb_ref[0, r0 + HEAD_DIM:r0 + V_ROWS, :] = jnp.ones((V_ROWS - HEAD_DIM, tm), BF16)
    else:
        q_ref[0] = (qn * (HEAD_DIM ** -0.5)).astype(BF16)
        qi_ref[0] = qi.astype(BF16)
        wi_ref[0] = wi
        ki2_ref[0] = kin.astype(BF16)
        vb_ref[0] = hv.astype(BF16)


def _project(x, p, u_hist, *, tm, with_pool):
    B, T, D = x.shape
    nt = T // tm
    tile = lambda w: pl.BlockSpec((1, tm, w), lambda b, i: (b, i, 0))
    const2 = lambda a: pl.BlockSpec(a.shape, lambda b, i: (0,) * a.ndim)
    ins = [x, p["attn_g"], p["w_in"], p["q_g"], p["k_g"], p["ki_g"], p["gmat"], p["pool_w"], p["pool_s"], u_hist]
    in_specs = [tile(D)] + [const2(a) for a in ins[1:]]
    sds = jax.ShapeDtypeStruct
    ttile = lambda r: pl.BlockSpec((1, r, tm), lambda b, i: (b, 0, i))
    if with_pool:
        v_rows = (512 // HEAD_DIM) * V_ROWS
        out_shape = [
            sds((B, T, 512), BF16),
            sds((B, 512, T), BF16),
            sds((B, T, 512), BF16),
            sds((B, v_rows, T), BF16),
            sds((B, 512, T), BF16),
            sds((B, T, IDX_DIM), BF16),
            sds((B, N_IDX_HEADS, T), F32),
        ]
        out_specs = [tile(512), ttile(512), tile(512), ttile(v_rows), ttile(512), tile(IDX_DIM),
                     ttile(N_IDX_HEADS)]
    else:
        out_shape = [
            sds((B, T, 512), F32),
            sds((B, T, 512), BF16),
            sds((B, T, 512), BF16),
            sds((B, T, 512), BF16),
            sds((B, T, 512), BF16),
            sds((B, T, 128), BF16),
            sds((B, T, 128), F32),
        ]
        out_specs = [tile(512)] * 5 + [tile(128), tile(128)]
    out_shape += [sds((B, T, 512), F32), sds((B, T, 512), F32), sds((B, T, IDX_DIM), F32),
                  sds((B, 16, 512), F32)]
    out_specs += [tile(512), tile(512), tile(IDX_DIM), pl.BlockSpec((1, 16, 512), lambda b, i: (b, 0, 0))]
    names = ["yp", "q", "kb", "vb", "qi", "ki2", "wi", "kf", "vf", "kif", "ulast"]
    outs = pl.pallas_call(
        functools.partial(_proj_kernel, with_pool=with_pool),
        grid=(B, nt),
        in_specs=in_specs,
        out_specs=out_specs,
        out_shape=out_shape,
        scratch_shapes=[pltpu.VMEM((tm + 16, 512), F32)],
        compiler_params=_cparams(("arbitrary", "arbitrary")),
        name="proj",
    )(*ins)
    return dict(zip(names, outs))


def _sortable_key(s):
    bits = pltpu.bitcast(s, I32)
    bits = jnp.where(bits == INT_MIN, 0, bits)
    return bits ^ ((bits >> 31) & 0x7FFFFFFF)


def _select_topk(key_ref, bias_ref, nc, topk, tq, kc):
    def count(pred):
        def body(c, acc):
            return acc + jnp.where(pred(c, key_ref[c]), 1.0, 0.0)

        acc = lax.fori_loop(0, nc, body, jnp.zeros((tq, kc), F32))
        return jnp.sum(acc, axis=1, keepdims=True)

    kf = float(topk)

    def bit_body(j, t_u):
        cand_u = t_u | jnp.left_shift(jnp.int32(1), 31 - j)
        cand_s = cand_u ^ INT_MIN
        cnt = count(lambda c, k: k >= cand_s)
        return jnp.where(cnt >= kf, cand_u, t_u)

    t_u = lax.fori_loop(0, 32, bit_body, jnp.zeros((tq, 1), I32))
    t = t_u ^ INT_MIN
    cnt_ge = count(lambda c, k: k >= t)
    has_tie = jnp.max(jnp.where(t > INT_MIN, cnt_ge, 0.0)) > kf

    def write_bias(sel_fn):
        def body(c, carry):
            k = key_ref[c]
            bias_ref[c] = jnp.where(sel_fn(c, k) & (k > INT_MIN), 0.0, NEG_BIG)
            return carry

        lax.fori_loop(0, nc, body, 0)

    @pl.when(jnp.logical_not(has_tie))
    def _():
        write_bias(lambda c, k: k >= t)

    @pl.when(has_tie)
    def _():
        lane = lax.broadcasted_iota(I32, (tq, kc), 1)
        need = kf - count(lambda c, k: k > t)

        def pos_body(j, p):
            cand = p | jnp.left_shift(jnp.int32(1), 15 - j)
            cnt = count(lambda c, k: (k == t) & (c * kc + lane < cand))
            return jnp.where(cnt <= need, cand, p)

        p = lax.fori_loop(0, 16, pos_body, jnp.zeros((tq, 1), I32))
        write_bias(lambda c, k: (k > t) | ((k == t) & (c * kc + lane < p)))


def _key_to_f32(ku):
    ks = ku ^ INT_MIN
    return lax.bitcast_convert_type(ks ^ ((ks >> 31) & 0x7FFFFFFF), F32)


def _attn_kernel(qT_ref, qiT_ref, wiT_ref, kb_ref, vT_ref, ki_ref, kbm_ref, vTm_ref, kim_ref, o_ref,
                 sc_ref, scm_ref, hi_ref, him_ref, bias_ref, biasm_ref, qm_ref, lg_ref, p_ref, acc_ref,
                 m_ref, l_ref, a_ref,
                 *, topk, n_meta):
    i = pl.program_id(1)
    tq = qT_ref.shape[2]
    kc = tq
    mp = kbm_ref.shape[0]
    nh = 512 // HEAD_DIM
    nc = i + 1
    kf = float(topk)
    ninf = -jnp.inf
    wiT = wiT_ref[0]
    rowm = lax.broadcasted_iota(I32, (mp, tq), 0)
    rowc = lax.broadcasted_iota(I32, (kc, tq), 0)
    colc = lax.broadcasted_iota(I32, (kc, tq), 1)

    def rows_of(c):
        return pl.ds(pl.multiple_of(c * kc, kc), kc)

    def scores(ki_chunk):
        s = None
        for h in range(N_IDX_HEADS):
            d = _dot(ki_chunk, qiT_ref[0, h * IDX_DIM:(h + 1) * IDX_DIM, :])
            t = jnp.maximum(d, 0.0) * wiT[h:h + 1, :]
            s = t if s is None else s + t
        return s

    def coarse(s):
        bits = jnp.where(s == 0.0, 0, lax.bitcast_convert_type(s, I32))
        return lax.bitcast_convert_type(bits & -65536, F32).astype(BF16)

    def put_scores(s, s_dst, h_dst):
        s_dst[...] = s
        h_dst[...] = coarse(s)

    put_scores(jnp.where(rowm >= mp - n_meta, scores(kim_ref[...]), ninf), scm_ref, him_ref)

    def score_body(c, carry):
        put_scores(scores(ki_ref[0, rows_of(c), :]), sc_ref.at[c], hi_ref.at[c])
        return carry

    lax.fori_loop(0, i, score_body, 0)
    put_scores(jnp.where(rowc <= colc, scores(ki_ref[0, rows_of(i), :]), ninf), sc_ref.at[i], hi_ref.at[i])
    put_scores(jnp.full((kc, tq), ninf, F32), sc_ref.at[nc], hi_ref.at[nc])

    half = (nc + 1) // 2

    def count_coarse(cand):
        one = jnp.ones((), BF16)
        zero = jnp.zeros((), BF16)

        def hits(hv):
            m = jnp.where(hv >= cand, one, zero)
            parts = [m[r * 16:(r + 1) * 16] for r in range(hv.shape[0] // 16)]
            while len(parts) > 1:
                parts = [parts[a] + parts[a + 1] for a in range(0, len(parts), 2)]
            return parts[0]

        def body(c, acc):
            return acc + (hits(hi_ref[c]) + hits(hi_ref[c + half])).astype(F32)

        acc = lax.fori_loop(0, half, body, hits(him_ref[...]).astype(F32))
        return jnp.sum(acc, axis=0, keepdims=True)

    def count(pred):
        def hits(s, p0):
            return jnp.sum(jnp.where(pred(s, p0), 1.0, 0.0).reshape(s.shape[0] // 8, 8, tq), axis=0)

        def body(c, acc):
            c2 = c + half
            return acc + hits(sc_ref[c], mp + c * kc) + hits(sc_ref[c2], mp + c2 * kc)

        return jnp.sum(lax.fori_loop(0, half, body, hits(scm_ref[...], 0)), axis=0, keepdims=True)

    def coarse_bit(j, t_u):
        cand_u = t_u | jnp.left_shift(jnp.int32(1), 31 - j)
        cand_bits = lax.bitcast_convert_type(_key_to_f32(cand_u), I32) & -65536
        cand = lax.bitcast_convert_type(cand_bits, F32).astype(BF16)
        return jnp.where(count_coarse(cand) >= kf, cand_u, t_u)

    def fine_bit(j, t_u):
        cand_u = t_u | jnp.left_shift(jnp.int32(1), 31 - j)
        cand = _key_to_f32(cand_u)
        cnt = count(lambda s, p0: s >= cand)
        return jnp.where(cnt >= kf, cand_u, t_u)

    t_u = lax.fori_loop(0, 16, coarse_bit, jnp.zeros((1, tq), I32))
    t_u = lax.fori_loop(16, 32, fine_bit, t_u)
    lane1 = lax.broadcasted_iota(I32, (1, tq), 1)
    n_valid = (i * tq + lane1 + (1 + n_meta)).astype(F32)
    t = jnp.where(n_valid > kf, _key_to_f32(t_u), -F32_MAX)
    cnt_ge = count(lambda s, p0: s >= t)
    has_tie = jnp.max(jnp.where(n_valid > kf, cnt_ge, 0.0)) > kf

    def write_bias(sel):
        biasm_ref[...] = jnp.where(sel(scm_ref[...], 0), 0.0, NEG_BIG)

        def body(c, carry):
            bias_ref[c] = jnp.where(sel(sc_ref[c], mp + c * kc), 0.0, NEG_BIG)
            return carry

        lax.fori_loop(0, nc, body, 0)

    @pl.when(jnp.logical_not(has_tie))
    def _():
        write_bias(lambda s, p0: s >= t)

    @pl.when(has_tie)
    def _():
        need = kf - count(lambda s, p0: s > t)

        def pos_of(s, p0):
            return p0 + lax.broadcasted_iota(I32, s.shape, 0)

        def pos_body(j, p):
            cand = p | jnp.left_shift(jnp.int32(1), 15 - j)
            cnt = count(lambda s, p0: (s == t) & (pos_of(s, p0) < cand))
            return jnp.where(cnt <= need, cand, p)

        p = lax.fori_loop(0, 16, pos_body, jnp.zeros((1, tq), I32))
        write_bias(lambda s, p0: (s > t) | ((s == t) & (pos_of(s, p0) < p)))

    sub = lax.broadcasted_iota(I32, (LANES, tq), 0)
    for h in range(nh):
        blk = qT_ref[0, (h // 2) * LANES:(h // 2 + 1) * LANES, :]
        keep = (sub < HEAD_DIM) if h % 2 == 0 else (sub >= HEAD_DIM)
        qm_ref[h] = jnp.where(keep, blk, jnp.zeros_like(blk))
    m_ref[...] = jnp.full(m_ref.shape, NEG_BIG, F32)
    l_ref[...] = jnp.zeros(l_ref.shape, F32)
    acc_ref[...] = jnp.zeros(acc_ref.shape, F32)

    def attend(k_of, v_of, bias, n):
        for h in range(nh):
            pr = slice((h // 2) * LANES, (h // 2 + 1) * LANES)
            lg = _dot(k_of(pr), qm_ref[h]) + bias
            lg_ref[h, 0:n, :] = lg
            m_old = m_ref[h:h + 1, :]
            m_new = jnp.maximum(m_old, jnp.max(lg, axis=0, keepdims=True))
            a_ref[h:h + 1, :] = jnp.exp2(m_old - m_new)
            m_ref[h:h + 1, :] = m_new
        for h in range(nh):
            p_ref[h, 0:n, :] = jnp.exp2(lg_ref[h, 0:n, :] - m_ref[h:h + 1, :]).astype(BF16)
        for h in range(nh):
            pv = _dot(v_of(slice(h * V_ROWS, (h + 1) * V_ROWS)), p_ref[h, 0:n, :])
            hs = slice(h * HEAD_DIM, (h + 1) * HEAD_DIM)
            alpha = a_ref[h:h + 1, :]
            acc_ref[hs, :] = alpha * acc_ref[hs, :] + pv[0:HEAD_DIM, :]
            l_ref[h:h + 1, :] = alpha * l_ref[h:h + 1, :] + pv[HEAD_DIM:HEAD_DIM + 1, :]

    attend(lambda pr: kbm_ref[:, pr], lambda vs: vTm_ref[vs, :], biasm_ref[...], mp)

    def attn_body(c, carry):
        rows = rows_of(c)
        attend(lambda pr: kb_ref[0, rows, pr], lambda vs: vT_ref[0, vs, rows], bias_ref[c], kc)
        return carry

    lax.fori_loop(0, nc, attn_body, 0)

    for h in range(nh):
        hs = slice(h * HEAD_DIM, (h + 1) * HEAD_DIM)
        acc_ref[hs, :] = acc_ref[hs, :] / l_ref[h:h + 1, :]
    o_ref[0] = acc_ref[...].T.astype(BF16)


def _prompt_attention(pp, pm, *, tq, topk):
    B, T, _ = pp["kb"].shape
    nq = T // tq
    nh = 512 // HEAD_DIM
    mp = pm["kb"].shape[0]
    ttile = lambda r: pl.BlockSpec((1, r, tq), lambda b, i: (b, 0, i))
    whole = lambda a: pl.BlockSpec((1,) + a.shape[1:], lambda b, i: (b, 0, 0))
    const = lambda a: pl.BlockSpec(a.shape, lambda b, i: (0, 0))
    return pl.pallas_call(
        functools.partial(_attn_kernel, topk=topk, n_meta=N_META),
        grid=(B, nq),
        in_specs=[ttile(512), ttile(512), ttile(N_IDX_HEADS), whole(pp["kb"]), whole(pp["vb"]),
                  whole(pp["ki2"]), const(pm["kb"]), const(pm["vb"]), const(pm["ki2"])],
        out_specs=pl.BlockSpec((1, tq, 512), lambda b, i: (b, i, 0)),
        out_shape=jax.ShapeDtypeStruct((B, T, 512), BF16),
        scratch_shapes=[pltpu.VMEM((nq + 1, tq, tq), F32), pltpu.VMEM((mp, tq), F32),
                        pltpu.VMEM((nq + 1, tq, tq), BF16), pltpu.VMEM((mp, tq), BF16),
                        pltpu.VMEM((nq, tq, tq), F32), pltpu.VMEM((mp, tq), F32),
                        pltpu.VMEM((nh, LANES, tq), BF16),
                        pltpu.VMEM((nh, tq, tq), F32), pltpu.VMEM((nh, tq, tq), BF16),
                        pltpu.VMEM((512, tq), F32),
                        pltpu.VMEM((nh, tq), F32), pltpu.VMEM((nh, tq), F32), pltpu.VMEM((nh, tq), F32)],
        compiler_params=_cparams(("arbitrary", "arbitrary")),
        name="attn",
    )(pp["q"], pp["qi"], pp["wi"], pp["kb"], pp["vb"], pp["ki2"], pm["kb"], pm["vb"], pm["ki2"])


def _split_dot3(x, whi_ref, wlo_ref):
    xh = x.astype(BF16)
    xl = (x - xh.astype(F32)).astype(BF16)
    return _dot(xh, whi_ref[...]) + _dot(xl, whi_ref[...]) + _dot(xh, wlo_ref[...])


def _mix_router_kernel(yp_ref, o_ref, x_ref, wo_ref, g_ref, rwh_ref, rwl_ref, rb_ref,
                       x1_ref, xn_ref, gate_ref, cnt_ref, carry_ref):
    @pl.when(pl.program_id(0) == 0)
    def _():
        carry_ref[...] = jnp.zeros(carry_ref.shape, F32)

    mix = _dot(yp_ref[...], wo_ref[0:512, :]) + _dot(o_ref[...], wo_ref[512:1024, :])
    x1 = x_ref[...] + mix
    x1_ref[...] = x1
    ms = jnp.mean(x1 * x1, axis=-1, keepdims=True)
    xn = x1 * lax.rsqrt(ms + EPS) * g_ref[...]
    xn_ref[...] = xn.astype(BF16)

    lg = _split_dot3(xn, rwh_ref, rwl_ref) + rb_ref[...]
    tm = lg.shape[0]
    lane = lax.broadcasted_iota(I32, (tm, LANES), 1)
    ninf = -jnp.inf
    gl = jnp.where(lane < N_EXPERT_GROUPS, lg, ninf)
    gmax = jnp.max(gl, axis=1, keepdims=True)
    p_sel = 1.0 / jnp.sum(jnp.exp(gl - gmax), axis=1, keepdims=True)
    g_sel = jnp.min(jnp.where(gl == gmax, lane, LANES), axis=1, keepdims=True)
    e0 = N_EXPERT_GROUPS + g_sel * EXPERTS_PER_GROUP
    ev = jnp.where((lane >= e0) & (lane < e0 + EXPERTS_PER_GROUP), lg, ninf)
    m1 = jnp.max(ev, axis=1, keepdims=True)
    i1 = jnp.min(jnp.where(ev == m1, lane, LANES), axis=1, keepdims=True)
    ev2 = jnp.where(lane == i1, ninf, ev)
    m2 = jnp.max(ev2, axis=1, keepdims=True)
    i2 = jnp.min(jnp.where(ev2 == m2, lane, LANES), axis=1, keepdims=True)
    e2 = jnp.exp(m2 - m1)
    inv = p_sel / (1.0 + e2)
    gates = jnp.where(lane == i1, inv, 0.0) + jnp.where(lane == i2, e2 * inv, 0.0)

    onehot = jnp.where(lane == g_sel, 1.0, 0.0)
    r_i = lax.broadcasted_iota(I32, (tm, tm), 0)
    c_i = lax.broadcasted_iota(I32, (tm, tm), 1)
    before = jnp.where(c_i < r_i, 1.0, 0.0).astype(BF16)
    prefix = _dot(before, onehot.astype(BF16)) + carry_ref[...]
    rank = jnp.sum(onehot * prefix, axis=1, keepdims=True)
    carry_ref[...] += jnp.sum(onehot, axis=0, keepdims=True)
    cnt_ref[...] = jnp.broadcast_to(carry_ref[...], cnt_ref.shape)
    gate_ref[...] = gates + jnp.where(lane == 0, g_sel.astype(F32), 0.0) + jnp.where(lane == 1, rank, 0.0)


def _mix_router(yp, o, x, p, *, tm):
    N, D = x.shape
    row = lambda w: pl.BlockSpec((tm, w), lambda i: (i, 0))
    const = lambda a: pl.BlockSpec(a.shape, lambda i: (0,) * a.ndim)
    consts = [p["w_out"], p["ffn_g"], p["rw_hi"], p["rw_lo"], p["rb"]]
    return pl.pallas_call(
        _mix_router_kernel,
        grid=(N // tm,),
        in_specs=[row(512), row(512), row(D)] + [const(a) for a in consts],
        out_specs=[row(D), row(D), row(LANES), pl.BlockSpec((8, LANES), lambda i: (0, 0))],
        out_shape=[jax.ShapeDtypeStruct((N, D), F32), jax.ShapeDtypeStruct((N, D), BF16),
                   jax.ShapeDtypeStruct((N, LANES), F32), jax.ShapeDtypeStruct((8, LANES), F32)],
        scratch_shapes=[pltpu.VMEM((1, LANES), F32)],
        compiler_params=_cparams(("arbitrary",)),
        name="mix_router",
    )(yp, o, x, *consts)


def _moe_kernel(xn_ref, gate_ref, x1_ref, wg_ref, wu_ref, wd_ref, y_ref, acc_ref):
    e = pl.program_id(1)

    @pl.when(e == 0)
    def _():
        acc_ref[...] = jnp.zeros_like(acc_ref)

    xn = xn_ref[...]
    hg = _dot(xn, wg_ref[0])
    hu = _dot(xn, wu_ref[0])
    hact = (hg * (1.0 / (1.0 + jnp.exp(-hg))) * hu).astype(BF16)
    d = _dot(hact, wd_ref[0])
    lane = lax.broadcasted_iota(I32, gate_ref.shape, 1)
    gcol = jnp.sum(jnp.where(lane == e + N_EXPERT_GROUPS, gate_ref[...], 0.0), axis=1, keepdims=True)
    acc_ref[...] += gcol * d

    @pl.when(e == pl.num_programs(1) - 1)
    def _():
        y_ref[...] = x1_ref[...] + acc_ref[...]


def _moe(xn, gate, x1, p, *, tm):
    N, D = x1.shape
    E, _, FF = p["w_gate"].shape
    row = lambda w: pl.BlockSpec((tm, w), lambda i, e: (i, 0))
    return pl.pallas_call(
        _moe_kernel,
        grid=(N // tm, E),
        in_specs=[row(D), row(LANES), row(D),
                  pl.BlockSpec((1, D, FF), lambda i, e: (e, 0, 0)),
                  pl.BlockSpec((1, D, FF), lambda i, e: (e, 0, 0)),
                  pl.BlockSpec((1, FF, D), lambda i, e: (e, 0, 0))],
        out_specs=row(D),
        out_shape=jax.ShapeDtypeStruct((N, D), F32),
        scratch_shapes=[pltpu.VMEM((tm, D), F32)],
        compiler_params=_cparams(("arbitrary", "arbitrary")),
        name="moe",
    )(xn, gate, x1, p["w_gate"], p["w_up"], p["w_down"])


def _dispatch_kernel(pos_ref, x1_ref, gate_ref, xs_hbm, row_ref, sem):
    tb, d = x1_ref.shape

    @pl.when(pl.program_id(0) == 0)
    def _():
        row_ref[...] = jnp.zeros(row_ref.shape, F32)

        def fill(j):
            return pltpu.make_async_copy(row_ref, xs_hbm.at[pl.ds(pl.multiple_of(j * tb, tb), tb)], sem.at[1])

        def fill_start(j, carry):
            fill(j).start()
            return carry

        def fill_wait(j, carry):
            fill(j).wait()
            return carry

        lax.fori_loop(0, xs_hbm.shape[0] // tb, fill_start, 0)
        lax.fori_loop(0, xs_hbm.shape[0] // tb, fill_wait, 0)

    row_ref[:, 0:d] = x1_ref[...]
    row_ref[:, d:] = gate_ref[...]

    def copy(r):
        return pltpu.make_async_copy(row_ref.at[pl.ds(r, 1)], xs_hbm.at[pl.ds(pos_ref[0, 0, r], 1)], sem.at[0])

    def start_body(r, carry):
        copy(r).start()
        return carry

    lax.fori_loop(0, tb, start_body, 0, unroll=8)

    def wait_body(r, carry):
        copy(r).wait()
        return carry

    lax.fori_loop(0, tb, wait_body, 0, unroll=8)


def _dispatch(pos3, x1, gate, n_pad):
    N, D = x1.shape
    nt, _, tb = pos3.shape
    w = D + LANES
    return pl.pallas_call(
        _dispatch_kernel,
        grid=(nt,),
        in_specs=[pl.BlockSpec((1, 1, tb), lambda i: (i, 0, 0), memory_space=pltpu.SMEM),
                  pl.BlockSpec((tb, D), lambda i: (i, 0)),
                  pl.BlockSpec((tb, LANES), lambda i: (i, 0))],
        out_specs=pl.BlockSpec(memory_space=pl.ANY),
        out_shape=jax.ShapeDtypeStruct((n_pad, w), F32),
        scratch_shapes=[pltpu.VMEM((tb, w), F32), pltpu.SemaphoreType.DMA((2,))],
        compiler_params=_cparams(("arbitrary",)),
        name="moe_dispatch",
    )(pos3, x1, gate)


def _grouped_moe_kernel(bg_ref, bv_ref, xs_ref, g_ref, wg_ref, wu_ref, wd_ref, ys_ref):
    j = pl.program_id(0)
    d = ys_ref.shape[1]

    @pl.when(bv_ref[j] == 0)
    def _():
        ys_ref[...] = jnp.zeros(ys_ref.shape, F32)

    @pl.when(bv_ref[j] != 0)
    def _():
        x1 = xs_ref[:, 0:d]
        gate = xs_ref[:, d:]
        ms = jnp.mean(x1 * x1, axis=-1, keepdims=True)
        xn = (x1 * lax.rsqrt(ms + EPS) * g_ref[...]).astype(BF16)
        lane = lax.broadcasted_iota(I32, gate.shape, 1)
        first = N_EXPERT_GROUPS + bg_ref[j] * EXPERTS_PER_GROUP
        acc = None
        for e in range(EXPERTS_PER_GROUP):
            hg = _dot(xn, wg_ref[e])
            hu = _dot(xn, wu_ref[e])
            hact = (hg * (1.0 / (1.0 + jnp.exp(-hg))) * hu).astype(BF16)
            gcol = jnp.sum(jnp.where(lane == first + e, gate, 0.0), axis=1, keepdims=True)
            t = gcol * _dot(hact, wd_ref[e])
            acc = t if acc is None else acc + t
        ys_ref[...] = acc


def _grouped_moe(blk_group, blk_valid, xs, p, *, tb):
    n_pad, w = xs.shape
    E, D, FF = p["w_gate"].shape
    grp = lambda a, b: pl.BlockSpec((EXPERTS_PER_GROUP, a, b), lambda j, bg, bv: (bg[j], 0, 0))
    grid_spec = pltpu.PrefetchScalarGridSpec(
        num_scalar_prefetch=2,
        grid=(n_pad // tb,),
        in_specs=[pl.BlockSpec((tb, w), lambda j, bg, bv: (j, 0)),
                  pl.BlockSpec((1, D), lambda j, bg, bv: (0, 0)),
                  grp(D, FF), grp(D, FF), grp(FF, D)],
        out_specs=pl.BlockSpec((tb, D), lambda j, bg, bv: (j, 0)),
    )
    return pl.pallas_call(
        _grouped_moe_kernel,
        grid_spec=grid_spec,
        out_shape=jax.ShapeDtypeStruct((n_pad, D), F32),
        compiler_params=_cparams(("arbitrary",)),
        name="moe_grouped",
    )(blk_group, blk_valid, xs, p["ffn_g"], p["w_gate"], p["w_up"], p["w_down"])


def _combine_kernel(pos_ref, x1_ref, ys_hbm, y_ref, buf_ref, sem):
    tb = x1_ref.shape[0]

    def copy(r):
        return pltpu.make_async_copy(ys_hbm.at[pl.ds(pos_ref[0, 0, r], 1)], buf_ref.at[pl.ds(r, 1)], sem.at[0])

    def start_body(r, carry):
        copy(r).start()
        return carry

    lax.fori_loop(0, tb, start_body, 0, unroll=8)

    def wait_body(r, carry):
        copy(r).wait()
        return carry

    lax.fori_loop(0, tb, wait_body, 0, unroll=8)
    y_ref[...] = x1_ref[...] + buf_ref[...]


def _combine(pos3, x1, ys):
    N, D = x1.shape
    nt, _, tb = pos3.shape
    return pl.pallas_call(
        _combine_kernel,
        grid=(nt,),
        in_specs=[pl.BlockSpec((1, 1, tb), lambda i: (i, 0, 0), memory_space=pltpu.SMEM),
                  pl.BlockSpec((tb, D), lambda i: (i, 0)),
                  pl.BlockSpec(memory_space=pl.ANY)],
        out_specs=pl.BlockSpec((tb, D), lambda i: (i, 0)),
        out_shape=jax.ShapeDtypeStruct((N, D), F32),
        scratch_shapes=[pltpu.VMEM((tb, D), F32), pltpu.SemaphoreType.DMA((1,))],
        compiler_params=_cparams(("arbitrary",)),
        name="moe_combine",
    )(pos3, x1, ys)


def _moe_grouped_path(x1, gate, counts, p, *, tb):
    N, D = x1.shape
    cnt = counts[0, :N_EXPERT_GROUPS].astype(I32)
    padded = ((cnt + tb - 1) // tb) * tb
    goff = jnp.cumsum(padded) - padded
    n_pad = N + N_EXPERT_GROUPS * tb
    grp = gate[:, 0].astype(I32)
    pos = (goff[grp] + gate[:, 1].astype(I32)).reshape(N // tb, 1, tb)
    starts = jnp.arange(n_pad // tb, dtype=I32) * tb
    blk_group = jnp.clip(jnp.sum(starts[:, None] >= (goff + padded)[None, :], axis=1), 0,
                         N_EXPERT_GROUPS - 1).astype(I32)
    blk_valid = (starts < (goff + cnt)[blk_group]).astype(I32)
    xs = _dispatch(pos, x1, gate, n_pad)
    ys = _grouped_moe(blk_group, blk_valid, xs, p, tb=tb)
    return _combine(pos, x1, ys)


def _dec_pool_kernel(st_ref, u_ref, pw_ref, ps_ref, yp_ref):
    n_prev = st_ref.shape[0]
    u = u_ref[...]
    for g, w in enumerate(POOL_WINDOWS):
        cs = slice(g * LANES, (g + 1) * LANES)
        acc = u[:, cs]
        for j in range(1, w):
            acc = acc + st_ref[n_prev - j, :, cs]
        m = (acc * (1.0 / w) - u[:, cs]).astype(BF16)
        yp_ref[:, cs] = (_dot(m, pw_ref[g]) * ps_ref[:, cs]).astype(BF16)


def _dec_pool(state_t, u, p):
    return pl.pallas_call(
        _dec_pool_kernel,
        out_shape=jax.ShapeDtypeStruct(u.shape, BF16),
        compiler_params=pltpu.CompilerParams(vmem_limit_bytes=VMEM_LIMIT),
        name="dec_pool",
    )(state_t, u, p["pool_w"], p["pool_s"])


def _dec_scores_kernel(pt_ref, qi_ref, w_ref, kin_ref, kidx_hbm, out_ref, buf, sem, *, n_pages, page, kc):
    b = pl.program_id(0)
    nb = pl.num_programs(0)

    def copies(bb, slot):
        return [pltpu.make_async_copy(kidx_hbm.at[0, pt_ref[bb, pg]],
                                      buf.at[slot, :, pl.ds(pg * page, page)], sem.at[slot])
                for pg in range(n_pages)]

    @pl.when(b == 0)
    def _():
        for c in copies(0, 0):
            c.start()

    @pl.when(b + 1 < nb)
    def _():
        for c in copies(b + 1, (b + 1) % 2):
            c.start()

    slot = b % 2
    for c in copies(b, slot):
        c.wait()

    qi = qi_ref[0]
    w = w_ref[0]
    past = n_pages * page
    dots = _dot(qi, buf[slot].astype(BF16))
    s = jnp.sum(jnp.maximum(dots, 0.0) * w, axis=0, keepdims=True)
    for c in range(past // kc):
        out_ref[c, pl.ds(b, 1), :] = s[:, c * kc:(c + 1) * kc]
    d_new = jnp.sum(qi.astype(F32) * kin_ref[0].astype(F32), axis=1, keepdims=True)
    s_new = jnp.sum(jnp.maximum(d_new, 0.0) * w, axis=0, keepdims=True)
    lane = lax.broadcasted_iota(I32, (1, kc), 1)
    out_ref[past // kc, pl.ds(b, 1), :] = jnp.where(lane == 0, s_new, -jnp.inf)


def _dec_scores(page_table, qi3, w3, kin3, cache_kidx_t, *, kc):
    DB, n_pages = page_table.shape
    page = cache_kidx_t.shape[3]
    past = n_pages * page
    nc1 = past // kc + 1
    grid_spec = pltpu.PrefetchScalarGridSpec(
        num_scalar_prefetch=1,
        grid=(DB,),
        in_specs=[pl.BlockSpec((1, N_IDX_HEADS, IDX_DIM), lambda b, pt: (b, 0, 0)),
                  pl.BlockSpec((1, N_IDX_HEADS, 1), lambda b, pt: (b, 0, 0)),
                  pl.BlockSpec((1, 1, IDX_DIM), lambda b, pt: (b, 0, 0)),
                  pl.BlockSpec(memory_space=pl.ANY)],
        out_specs=pl.BlockSpec((nc1, DB, kc), lambda b, pt: (0, 0, 0)),
        scratch_shapes=[pltpu.VMEM((2, IDX_DIM, past), F32), pltpu.SemaphoreType.DMA((2,))],
    )
    return pl.pallas_call(
        functools.partial(_dec_scores_kernel, n_pages=n_pages, page=page, kc=kc),
        grid_spec=grid_spec,
        out_shape=jax.ShapeDtypeStruct((nc1, DB, kc), F32),
        compiler_params=_cparams(("arbitrary",)),
        name="dec_scores",
    )(page_table, qi3, w3, kin3, cache_kidx_t)


def _dec_topk_kernel(sc_ref, bias_ref, key_ref, *, topk, past):
    nc1, db, kc = sc_ref.shape
    lane = lax.broadcasted_iota(I32, (db, kc), 1)

    def key_body(c, carry):
        key_ref[c] = jnp.where(c * kc + lane <= past, _sortable_key(sc_ref[c]), INT_MIN)
        return carry

    lax.fori_loop(0, nc1, key_body, 0)
    _select_topk(key_ref, bias_ref, nc1, topk, db, kc)


def _dec_topk(scores, *, topk, past):
    nc1, db, kc = scores.shape
    return pl.pallas_call(
        functools.partial(_dec_topk_kernel, topk=topk, past=past),
        out_shape=jax.ShapeDtypeStruct((nc1, db, kc), F32),
        scratch_shapes=[pltpu.VMEM((nc1, db, kc), I32)],
        compiler_params=pltpu.CompilerParams(vmem_limit_bytes=VMEM_LIMIT),
        name="dec_topk",
    )(scores)


def _dec_attn_kernel(pt_ref, bias_ref, q_ref, kn_ref, vn_ref, ck_hbm, cv_hbm, o_ref,
                     ring, sem, qb_ref, lg_ref, p_ref, acc_ref, *, n_pages):
    b = pl.program_id(0)
    nb = pl.num_programs(0)
    depth, nh, hd, page = ring.shape

    def page_copy(src_hbm, bb, j, slot):
        return pltpu.make_async_copy(src_hbm.at[0, pt_ref[bb, j]], ring.at[slot], sem.at[slot])

    @pl.when(b == 0)
    def _():
        for j in range(depth):
            page_copy(ck_hbm, 0, j, j).start()

    q = q_ref[0].astype(F32)
    qb_ref[...] = jnp.broadcast_to(q, (nh, hd, page))

    def run(lo, hi, compute, prefetch):
        def body(j, carry):
            slot = lax.rem(j, depth)
            page_copy(ck_hbm, b, j, slot).wait()
            compute(j, slot)
            prefetch(j, slot)
            return carry

        lax.fori_loop(lo, hi, body, 0)

    def key_page(j, slot):
        lg_ref[j] = jnp.sum(ring[slot] * qb_ref[...], axis=1, keepdims=True)

    run(0, n_pages - depth, key_page, lambda j, s: page_copy(ck_hbm, b, j + depth, s).start())
    run(n_pages - depth, n_pages, key_page,
        lambda j, s: page_copy(cv_hbm, b, j + depth - n_pages, s).start())

    bias = bias_ref[:, pl.ds(b, 1), :]
    lg = lg_ref[...] + bias[0:n_pages][:, None]
    l_new = jnp.sum(q * kn_ref[0].astype(F32), axis=1, keepdims=True)
    l_new = jnp.where(bias[n_pages][:, 0:1] == 0.0, l_new, NEG_BIG)
    m = jnp.maximum(jnp.max(jnp.max(lg, axis=0), axis=2, keepdims=True), l_new)
    p = jnp.exp(lg - m[None])
    p_new = jnp.exp(l_new - m)
    p_ref[...] = p
    denom = jnp.sum(jnp.sum(p, axis=0), axis=2, keepdims=True) + p_new
    acc_ref[...] = jnp.zeros(acc_ref.shape, F32)

    def value_page(j, slot):
        acc_ref[...] += p_ref[j] * ring[slot]

    run(0, n_pages - depth, value_page, lambda j, s: page_copy(cv_hbm, b, j + depth, s).start())

    def next_keys(j, s):
        @pl.when(b + 1 < nb)
        def _():
            page_copy(ck_hbm, b + 1, j + depth - n_pages, s).start()

    run(n_pages - depth, n_pages, value_page, next_keys)

    out = jnp.sum(acc_ref[...], axis=2, keepdims=True) + p_new * vn_ref[0].astype(F32)
    o_ref[0] = (out / denom).astype(BF16)


def _dec_attn(page_table, bias, q4, kn4, vn4, cache_k_t, cache_v_t):
    DB, n_pages = page_table.shape
    _, _, nh, hd, page = cache_k_t.shape
    depth = min(DEC_RING, n_pages)
    assert n_pages % depth == 0 and bias.shape == (n_pages + 1, DB, page)
    col = pl.BlockSpec((1, nh, hd, 1), lambda b, pt: (b, 0, 0, 0))
    grid_spec = pltpu.PrefetchScalarGridSpec(
        num_scalar_prefetch=1,
        grid=(DB,),
        in_specs=[pl.BlockSpec(bias.shape, lambda b, pt: (0, 0, 0)), col, col, col,
                  pl.BlockSpec(memory_space=pl.ANY), pl.BlockSpec(memory_space=pl.ANY)],
        out_specs=col,
        scratch_shapes=[pltpu.VMEM((depth, nh, hd, page), F32), pltpu.SemaphoreType.DMA((depth,)),
                        pltpu.VMEM((nh, hd, page), F32),
                        pltpu.VMEM((n_pages, nh, 1, page), F32), pltpu.VMEM((n_pages, nh, 1, page), F32),
                        pltpu.VMEM((nh, hd, page), F32)],
    )
    return pl.pallas_call(
        functools.partial(_dec_attn_kernel, n_pages=n_pages),
        grid_spec=grid_spec,
        out_shape=jax.ShapeDtypeStruct((DB, nh, hd, 1), BF16),
        compiler_params=_cparams(("arbitrary",)),
        name="dec_attn",
    )(page_table, bias, q4, kn4, vn4, cache_k_t, cache_v_t)


def _prep_params(w_in, w_out, attn_norm, ffn_norm, q_norm, k_norm, kidx_norm, pool_w, pool_scale,
                 router_group_w, router_group_b, router_expert_w, router_expert_b,
                 expert_w_gate, expert_w_up, expert_w_down):
    d = w_in.shape[0]
    off_ki = 512 * 5
    w_main = w_in[:, :off_ki]
    w_ki = w_in[:, off_ki:off_ki + IDX_DIM]
    w_wi = w_in[:, off_ki + IDX_DIM:]
    w_cat = jnp.concatenate(
        [w_main, w_ki, w_ki, w_wi, jnp.zeros((d, LANES - w_wi.shape[1]), w_in.dtype)], axis=1)
    n_heads = 512 // HEAD_DIM
    gidx = jnp.arange(256) // HEAD_DIM
    gmat = (gidx[:, None] == gidx[None, :]).astype(F32) * (1.0 / HEAD_DIM)
    rw = jnp.concatenate([router_group_w, router_expert_w], axis=1)
    rw = jnp.pad(rw, ((0, 0), (0, LANES - rw.shape[1])))
    rw_hi = rw.astype(BF16)
    rb = jnp.concatenate([router_group_b, router_expert_b])
    rb = jnp.pad(rb, (0, LANES - rb.shape[0]))[None, :]
    return dict(
        w_in=w_cat.astype(BF16),
        attn_g=attn_norm[None, :],
        q_g=jnp.tile(q_norm, n_heads)[None, :],
        k_g=jnp.tile(k_norm, n_heads)[None, :],
        ki_g=jnp.tile(kidx_norm, 2)[None, :],
        gmat=gmat.astype(BF16),
        pool_w=pool_w.astype(BF16),
        pool_s=pool_scale[None, :],
        w_out=w_out.astype(BF16),
        ffn_g=ffn_norm[None, :],
        rw_hi=rw_hi,
        rw_lo=(rw - rw_hi.astype(F32)).astype(BF16),
        rb=rb,
        w_gate=expert_w_gate.astype(BF16),
        w_up=expert_w_up.astype(BF16),
        w_down=expert_w_down.astype(BF16),
    )


def _pad_rows(a, n):
    return jnp.pad(a, ((0, n - a.shape[0]), (0, 0)))


def kernel(x_prompt, x_sample, cache_k, cache_v, cache_kidx, state_pool, page_table, meta_tokens, w_in, w_out, attn_norm, ffn_norm, q_norm, k_norm, kidx_norm, pool_w, pool_scale, router_group_w, router_group_b, router_expert_w, router_expert_b, expert_w_gate, expert_w_up, expert_w_down):
    B, S, D = x_prompt.shape
    DB, DS, _ = x_sample.shape
    depth = w_in.shape[0]
    assert depth == 1 and DS == 1
    n_heads = 512 // HEAD_DIM
    p = _prep_params(w_in[0], w_out[0], attn_norm[0], ffn_norm[0], q_norm[0], k_norm[0], kidx_norm[0],
                     pool_w[0], pool_scale[0], router_group_w[0], router_group_b[0],
                     router_expert_w[0], router_expert_b[0], expert_w_gate[0], expert_w_up[0],
                     expert_w_down[0])
    tq = min(256, S)
    tm_proj = min(512, S)
    topk_p = min(TOPK_MAX, (S + N_META) // 4)
    assert S % tq == 0 and S % tm_proj == 0, "prompt length must be a multiple of the tile sizes"

    zeros_hist = jnp.zeros((16, 512), F32)
    x_meta = jnp.concatenate([jnp.zeros((META_PAD - N_META, D), F32), meta_tokens.astype(F32)], axis=0)
    pm = _project(x_meta[None], p, zeros_hist, tm=META_PAD, with_pool=True)
    pp = _project(x_prompt, p, pm["ulast"][0], tm=tm_proj, with_pool=True)
    pmeta = {k: pm[k][0] for k in ("kb", "vb", "ki2")}
    o = _prompt_attention(pp, pmeta, tq=tq, topk=topk_p)
    n_tok = B * S
    tm_tok = min(512, n_tok)
    x1, _, gate, counts = _mix_router(pp["yp"].reshape(n_tok, 512), o.reshape(n_tok, 512),
                                      x_prompt.reshape(n_tok, D), p, tm=tm_tok)
    y_prompt = _moe_grouped_path(x1, gate, counts, p, tb=tm_tok).reshape(B, S, D)

    def with_meta(a_meta, a):
        a_meta = a_meta[:, META_PAD - N_META:]
        full = jnp.concatenate([jnp.broadcast_to(a_meta, (B,) + a_meta.shape[1:]), a], axis=1)
        return full

    new_k_prompt = with_meta(pm["kf"], pp["kf"]).reshape(1, B, S + N_META, n_heads, HEAD_DIM)
    new_v_prompt = with_meta(pm["vf"], pp["vf"]).reshape(1, B, S + N_META, n_heads, HEAD_DIM)
    new_kidx_prompt = with_meta(pm["kif"], pp["kif"])[None]
    new_pool_prompt = pp["ulast"][:, 1:, :][None]

    n_pages = page_table.shape[1]
    page = cache_k.shape[2]
    past = n_pages * page
    topk_s = min(TOPK_MAX, (past + DS) // 4)
    ps = _project(x_sample.reshape(1, DB, D), p, zeros_hist, tm=DB, with_pool=False)
    u_s = ps["yp"][0]
    yp_s = _dec_pool(jnp.swapaxes(state_pool[0], 0, 1), u_s, p)
    scores = _dec_scores(page_table, ps["qi"][0].reshape(DB, N_IDX_HEADS, IDX_DIM),
                         ps["wi"][0][:, :N_IDX_HEADS].reshape(DB, N_IDX_HEADS, 1),
                         ps["ki2"][0][:, :IDX_DIM].reshape(DB, 1, IDX_DIM),
                         jnp.swapaxes(cache_kidx, 2, 3), kc=page)
    bias_s = _dec_topk(scores, topk=topk_s, past=past)
    col = lambda a: a[0].reshape(DB, n_heads, HEAD_DIM, 1)
    o_s = _dec_attn(page_table, bias_s, col(ps["q"]), col(ps["kb"]), col(ps["vb"]),
                    jnp.transpose(cache_k, (0, 1, 3, 4, 2)), jnp.transpose(cache_v, (0, 1, 3, 4, 2)))
    x1_s, xn2_s, gate_s, _ = _mix_router(yp_s, o_s.reshape(DB, 512), x_sample.reshape(DB, D), p, tm=DB)
    y_sample = _moe(xn2_s, gate_s, x1_s, p, tm=DB).reshape(DB, DS, D)
    new_k_sample = ps["kf"].reshape(1, DB, DS, n_heads, HEAD_DIM)
    new_v_sample = ps["vf"].reshape(1, DB, DS, n_heads, HEAD_DIM)
    new_kidx_sample = ps["kif"].reshape(1, DB, DS, IDX_DIM)
    new_pool_sample = jnp.concatenate([state_pool[0][:, 1:, :], u_s[:, None, :]], axis=1)[None]
    return (y_prompt, y_sample, new_k_prompt, new_v_prompt, new_kidx_prompt, new_pool_prompt,
            new_k_sample, new_v_sample, new_kidx_sample, new_pool_sample)
```
